```python
import math
import jax, jax.numpy as jnp
from jax import lax
import numpy as np

D_MODEL = 2048
BATCH = 2
SEQ = 16384
DEPTH = 4

SG_HEADS = 8
SG_HEAD_DIM = 128
SG_WIDTH = SG_HEADS * SG_HEAD_DIM
SG_CHUNK = 128
RET_HEADS = 4
RET_HEAD_DIM = 256
RET_WIDTH = RET_HEADS * RET_HEAD_DIM
RET_CHUNK = 128
MIX_WIDTH = SG_WIDTH + RET_WIDTH
IN_WIDTH = 2 * SG_WIDTH + 4 * RET_WIDTH
ROPE_BASE = 10000.0
D_FF = 5632
N_EXPERTS = 8
TOP_K = 2
N_DENSE = (DEPTH + 1) // 2
N_MOE = DEPTH // 2
N_MOD = 6
EPS = 1e-6

kernel_name = "hybrid_gmlp_retention_moe_adaln"


def rms_norm(x, w):
    xf = x.astype(jnp.float32)
    y = xf * lax.rsqrt(jnp.mean(xf * xf, axis=-1, keepdims=True) + EPS)
    return (y * w.astype(jnp.float32)).astype(x.dtype)


def modulate(h, shift, scale):
    return h * (1 + scale[:, None, :]) + shift[:, None, :]


def rotary(t, pos):
    d = t.shape[-1]
    half = d // 2
    inv_freq = 1.0 / (ROPE_BASE ** (jnp.arange(half, dtype=jnp.float32) * 2.0 / d))
    ang = pos[:, None] * inv_freq[None, :]
    cos = jnp.cos(ang)[None, :, None, :]
    sin = jnp.sin(ang)[None, :, None, :]
    t1, t2 = t[..., :half], t[..., half:]
    return jnp.concatenate([t1 * cos - t2 * sin, t1 * sin + t2 * cos], axis=-1)


def spatial_gating(zu, zv, sg_w, sg_b):
    B, S, _ = zu.shape
    n = S // SG_CHUNK
    u = jax.nn.gelu(zu)
    v = jax.nn.gelu(zv).astype(jnp.float32).reshape(B, n, SG_CHUNK, SG_HEADS, SG_HEAD_DIM)
    mu = jnp.mean(v, axis=-1, keepdims=True)
    vc = v - mu
    v = vc * lax.rsqrt(jnp.mean(vc * vc, axis=-1, keepdims=True) + EPS)
    causal = jnp.tril(jnp.ones((SG_CHUNK, SG_CHUNK), jnp.float32))
    w = sg_w.astype(jnp.float32) * causal[None]
    v = jnp.einsum('hts,bnshd->bnthd', w, v) + sg_b.astype(jnp.float32).T[None, None, :, :, None]
    return u * v.reshape(B, S, SG_WIDTH).astype(u.dtype)


def retention(q, k, v, g):
    B, S, _ = q.shape
    H, dk, C = RET_HEADS, RET_HEAD_DIM, RET_CHUNK
    n = S // C
    pos = jnp.arange(S, dtype=jnp.float32)
    qf = rotary(q.astype(jnp.float32).reshape(B, S, H, dk), pos)
    kf = rotary(k.astype(jnp.float32).reshape(B, S, H, dk), pos) * (dk ** -0.5)
    vf = v.astype(jnp.float32).reshape(B, S, H, dk)
    log_gamma = jnp.log(1.0 - 2.0 ** (-5.0 - jnp.arange(H, dtype=jnp.float32)))
    j = jnp.arange(C, dtype=jnp.float32)
    diff = j[:, None] - j[None, :]
    dmask = jnp.where(diff >= 0, jnp.exp(log_gamma[:, None, None] * jnp.maximum(diff, 0.0)), 0.0)
    xi = jnp.exp(log_gamma[:, None] * (j[None, :] + 1.0))
    zeta = jnp.exp(log_gamma[:, None] * (C - 1.0 - j[None, :]))
    chunk_decay = jnp.exp(log_gamma * C)

    def to_chunks(t):
        return t.reshape(B, n, C, H, dk).transpose(1, 0, 3, 2, 4)

    def step(state, qkv):
        qc, kc, vc = qkv
        scores = jnp.einsum('bhcd,bhed->bhce', qc, kc) * dmask[None]
        y = (jnp.einsum('bhce,bhev->bhcv', scores, vc)
             + jnp.einsum('bhcd,bhdv->bhcv', qc, state) * xi[None, :, :, None])
        state = (state * chunk_decay[None, :, None, None]
                 + jnp.einsum('bhcd,bhcv->bhdv', kc * zeta[None, :, :, None], vc))
        return state, y

    state0 = jnp.zeros((B, H, dk, dk), jnp.float32)
    _, y = lax.scan(step, state0, (to_chunks(qf), to_chunks(kf), to_chunks(vf)))
    y = y.transpose(1, 0, 3, 2, 4).reshape(B, S, H, dk)
    y = y * lax.rsqrt(jnp.mean(y * y, axis=-1, keepdims=True) + EPS)
    y = y.reshape(B, S, RET_WIDTH)
    return (jax.nn.silu(g.astype(jnp.float32)) * y).astype(g.dtype)


def swiglu(h, wg, wu, wd):
    return (jax.nn.silu(h @ wg) * (h @ wu)) @ wd


def moe_swiglu(h, router_w, router_b, wg, wu, wd):
    logits = (h @ router_w).astype(jnp.float32) + router_b.astype(jnp.float32)
    top_val, top_idx = lax.top_k(logits, TOP_K)
    top_w = jax.nn.softmax(top_val, axis=-1)
    combine = jnp.sum(jax.nn.one_hot(top_idx, N_EXPERTS, dtype=jnp.float32) * top_w[..., None], axis=-2)
    combine = combine.astype(h.dtype)
    out = jnp.zeros_like(h)
    for e in range(N_EXPERTS):
        out = out + combine[..., e:e + 1] * swiglu(h, wg[e], wu[e], wd[e])
    return out


def setup_inputs(seed: int = 0) -> dict:
    key = jax.random.key(seed)
    ks = jax.random.split(key, 20)
    f32 = jnp.float32
    nrm = lambda k, shape, s: jax.random.normal(k, shape, f32) * s
    C = SG_CHUNK
    return {
        "x": nrm(ks[0], (BATCH, SEQ, D_MODEL), 1.0),
        "c": nrm(ks[1], (BATCH, D_MODEL), 1.0),
        "norm1_w": 1.0 + nrm(ks[2], (DEPTH, D_MODEL), 0.02),
        "ada_w": nrm(ks[3], (DEPTH, D_MODEL, N_MOD * D_MODEL), 0.5 * D_MODEL ** -0.5),
        "ada_b": nrm(ks[4], (DEPTH, N_MOD * D_MODEL), 0.02),
        "w_in": nrm(ks[5], (DEPTH, D_MODEL, IN_WIDTH), D_MODEL ** -0.5),
        "sg_w": nrm(ks[6], (DEPTH, SG_HEADS, C, C), C ** -0.5),
        "sg_b": 1.0 + nrm(ks[7], (DEPTH, SG_HEADS, C), 0.02),
        "w_out": nrm(ks[8], (DEPTH, MIX_WIDTH, D_MODEL), MIX_WIDTH ** -0.5),
        "norm2_w": 1.0 + nrm(ks[9], (DEPTH, D_MODEL), 0.02),
        "ffn_w_gate": nrm(ks[10], (N_DENSE, D_MODEL, D_FF), D_MODEL ** -0.5),
        "ffn_w_up": nrm(ks[11], (N_DENSE, D_MODEL, D_FF), D_MODEL ** -0.5),
        "ffn_w_down": nrm(ks[12], (N_DENSE, D_FF, D_MODEL), D_FF ** -0.5),
        "router_w": nrm(ks[13], (N_MOE, D_MODEL, N_EXPERTS), D_MODEL ** -0.5),
        "router_b": nrm(ks[14], (N_MOE, N_EXPERTS), 0.01),
        "moe_w_gate": nrm(ks[15], (N_MOE, N_EXPERTS, D_MODEL, D_FF), D_MODEL ** -0.5),
        "moe_w_up": nrm(ks[16], (N_MOE, N_EXPERTS, D_MODEL, D_FF), D_MODEL ** -0.5),
        "moe_w_down": nrm(ks[17], (N_MOE, N_EXPERTS, D_FF, D_MODEL), D_FF ** -0.5),
        "final_norm_w": 1.0 + nrm(ks[18], (D_MODEL,), 0.02),
    }


def reference(x, c, norm1_w, ada_w, ada_b, w_in, sg_w, sg_b, w_out, norm2_w,
              ffn_w_gate, ffn_w_up, ffn_w_down, router_w, router_b,
              moe_w_gate, moe_w_up, moe_w_down, final_norm_w):
    c_act = jax.nn.silu(c)
    splits = [SG_WIDTH, 2 * SG_WIDTH, 2 * SG_WIDTH + RET_WIDTH,
              2 * SG_WIDTH + 2 * RET_WIDTH, 2 * SG_WIDTH + 3 * RET_WIDTH]
    for l in range(DEPTH):
        mod = c_act @ ada_w[l] + ada_b[l]
        shift1, scale1, gate1, shift2, scale2, gate2 = jnp.split(mod, N_MOD, axis=-1)
        h = modulate(rms_norm(x, norm1_w[l]), shift1, scale1)
        z = h @ w_in[l]
        zu, zv, q, k, v, g = jnp.split(z, splits, axis=-1)
        y_sg = spatial_gating(zu, zv, sg_w[l], sg_b[l])
        y_ret = retention(q, k, v, g)
        y = jnp.concatenate([y_sg, y_ret], axis=-1) @ w_out[l]
        x = x + gate1[:, None, :] * y
        h2 = modulate(rms_norm(x, norm2_w[l]), shift2, scale2)
        i = l // 2
        if l % 2 == 0:
            f = swiglu(h2, ffn_w_gate[i], ffn_w_up[i], ffn_w_down[i])
        else:
            f = moe_swiglu(h2, router_w[i], router_b[i], moe_w_gate[i], moe_w_up[i], moe_w_down[i])
        x = x + gate2[:, None, :] * f
    return rms_norm(x, final_norm_w)
```

```python
import functools

import jax
import jax.numpy as jnp
from jax import lax
from jax.experimental import pallas as pl
from jax.experimental.pallas import tpu as pltpu

F32 = jnp.float32
BF16 = jnp.bfloat16

EPS = 1e-6
ROPE_BASE = 10000.0
SG_HEAD_DIM = 128
RET_HEAD_DIM = 256
CHUNK = 128
TOP_K = 2
LANES = 128

TM_INPROJ = 1024
TN_INPROJ = 1024
TC_MIXER = 512
TM_OUTPROJ = 512
TM_FFN = 512
TF_FFN = 512
TM_MOE = 512
TG_GATHER = 1024
TR_COMBINE = 256
TM_NORM = 1024
TN_MOD = 512
VMEM_LIMIT = 56 * 1024 * 1024


def _params(n_axes):
    return pltpu.CompilerParams(dimension_semantics=("arbitrary",) * n_axes,
                                vmem_limit_bytes=VMEM_LIMIT)


def _gelu_tanh(x):
    return 0.5 * x * (1.0 + jnp.tanh(0.7978845608028654 * (x + 0.044715 * (x * x * x))))


def _silu(x):
    return x * jax.nn.sigmoid(x)


def _rms_modulate(x, nw, shift, scale):
    y = x * lax.rsqrt(jnp.mean(x * x, axis=-1, keepdims=True) + EPS)
    return (y * nw) * (1.0 + scale) + shift


def _mod_kernel(cb_ref, w_ref, b_ref, o_ref):
    tn = w_ref.shape[1]
    for b in range(cb_ref.shape[0]):
        act = _silu(cb_ref[b])
        cols = [jnp.sum(w_ref[:, j * LANES:(j + 1) * LANES] * act, axis=0, keepdims=True)
                for j in range(tn // LANES)]
        o_ref[b:b + 1, :] = jnp.concatenate(cols, axis=1) + b_ref[...]


def _modulation(c, ada_w, ada_b):
    depth, d, n6 = ada_w.shape
    nb = c.shape[0]
    tn = min(TN_MOD, n6)
    cb = jnp.broadcast_to(c[:, :, None], (nb, d, LANES))
    return pl.pallas_call(
        _mod_kernel,
        out_shape=jax.ShapeDtypeStruct((depth, nb, n6), F32),
        grid=(depth, n6 // tn),
        in_specs=[pl.BlockSpec((nb, d, LANES), lambda l, j: (0, 0, 0)),
                  pl.BlockSpec((None, d, tn), lambda l, j: (l, 0, j)),
                  pl.BlockSpec((None, 1, tn), lambda l, j: (l, 0, j))],
        out_specs=pl.BlockSpec((None, nb, tn), lambda l, j: (l, 0, j)),
        compiler_params=_params(2),
        name="adaln_mod",
    )(cb, ada_w, ada_b.reshape(depth, 1, n6))


def _inproj_kernel(x_ref, nw_ref, mod_ref, w_ref, z_ref, h_ref):
    @pl.when(pl.program_id(1) == 0)
    def _():
        h = _rms_modulate(x_ref[...], nw_ref[...], mod_ref[0:1, :], mod_ref[1:2, :])
        h_ref[...] = h.astype(BF16)

    z_ref[...] = jnp.dot(h_ref[...], w_ref[...], preferred_element_type=F32).astype(z_ref.dtype)


def _inproj(x, nw, mod, w, seq):
    n, d = x.shape
    nin = w.shape[1]
    tm = min(TM_INPROJ, seq)
    tn = min(TN_INPROJ, nin)
    per_b = seq // tm
    return pl.pallas_call(
        _inproj_kernel,
        out_shape=jax.ShapeDtypeStruct((n, nin), BF16),
        grid=(n // tm, nin // tn),
        in_specs=[pl.BlockSpec((tm, d), lambda i, j: (i, 0)),
                  pl.BlockSpec((1, d), lambda i, j: (0, 0)),
                  pl.BlockSpec((None, 6, d), lambda i, j: (i // per_b, 0, 0)),
                  pl.BlockSpec((d, tn), lambda i, j: (0, j))],
        out_specs=pl.BlockSpec((tm, tn), lambda i, j: (i, j)),
        scratch_shapes=[pltpu.VMEM((tm, d), BF16)],
        compiler_params=_params(2),
        name="inproj",
    )(x, nw, mod, w)


def _mixer_kernel(z_ref, cos_ref, sin_ref, sgw_ref, sgb_ref, dmask_ref, xi_ref, zeta_ref,
                  dec_ref, y_ref, wm_ref, state_ref, *, sg_heads, ret_heads, steps_per_seq):
    sgw_w = sg_heads * SG_HEAD_DIM
    ret_w = ret_heads * RET_HEAD_DIM
    q_off, k_off = 2 * sgw_w, 2 * sgw_w + ret_w
    v_off, g_off = 2 * sgw_w + 2 * ret_w, 2 * sgw_w + 3 * ret_w
    half = RET_HEAD_DIM // 2
    step = pl.program_id(0)

    @pl.when(step == 0)
    def _():
        t = lax.broadcasted_iota(jnp.int32, (CHUNK, CHUNK), 0)
        s = lax.broadcasted_iota(jnp.int32, (CHUNK, CHUNK), 1)
        causal = (t >= s).astype(F32)
        for h in range(sg_heads):
            wm_ref[h] = (sgw_ref[h] * causal).astype(BF16)

    @pl.when(step % steps_per_seq == 0)
    def _():
        state_ref[...] = jnp.zeros_like(state_ref)

    def rotate(t, cos, sin):
        t1, t2 = t[:, :half], t[:, half:]
        return jnp.concatenate([t1 * cos - t2 * sin, t1 * sin + t2 * cos], axis=-1)

    def chunk(c, carry):
        rows = pl.ds(pl.multiple_of(c * CHUNK, CHUNK), CHUNK)
        cos = cos_ref[rows, :]
        sin = sin_ref[rows, :]
        for h in range(sg_heads):
            lo = h * SG_HEAD_DIM
            u = _gelu_tanh(z_ref[rows, lo:lo + SG_HEAD_DIM].astype(F32))
            v = _gelu_tanh(z_ref[rows, sgw_w + lo:sgw_w + lo + SG_HEAD_DIM].astype(F32))
            vc = v - jnp.mean(v, axis=-1, keepdims=True)
            vn = vc * lax.rsqrt(jnp.mean(vc * vc, axis=-1, keepdims=True) + EPS)
            sv = jnp.dot(wm_ref[h], vn.astype(BF16), preferred_element_type=F32) + sgb_ref[h]
            y_ref[rows, lo:lo + SG_HEAD_DIM] = (u * sv).astype(y_ref.dtype)
        for h in range(ret_heads):
            lo = h * RET_HEAD_DIM
            q = z_ref[rows, q_off + lo:q_off + lo + RET_HEAD_DIM].astype(F32)
            k = z_ref[rows, k_off + lo:k_off + lo + RET_HEAD_DIM].astype(F32)
            v = z_ref[rows, v_off + lo:v_off + lo + RET_HEAD_DIM]
            g = z_ref[rows, g_off + lo:g_off + lo + RET_HEAD_DIM].astype(F32)
            qr = rotate(q, cos, sin)
            kr = rotate(k, cos, sin) * (RET_HEAD_DIM ** -0.5)
            qb = qr.astype(BF16)
            scores = lax.dot_general(qb, kr.astype(BF16), (((1,), (1,)), ((), ())),
                                     preferred_element_type=F32) * dmask_ref[h]
            st = state_ref[h]
            yv = (jnp.dot(scores.astype(BF16), v, preferred_element_type=F32)
                  + jnp.dot(qb, st.astype(BF16), preferred_element_type=F32) * xi_ref[h])
            kz = (kr * zeta_ref[h]).astype(BF16)
            state_ref[h] = st * dec_ref[h] + lax.dot_general(
                kz, v, (((0,), (0,)), ((), ())), preferred_element_type=F32)
            yn = yv * lax.rsqrt(jnp.mean(yv * yv, axis=-1, keepdims=True) + EPS)
            y_ref[rows, sgw_w + lo:sgw_w + lo + RET_HEAD_DIM] = (_silu(g) * yn).astype(y_ref.dtype)
        return carry

    lax.fori_loop(0, z_ref.shape[0] // CHUNK, chunk, 0)


def _mixer(z, cos, sin, sgw, sgb, dmask, xi, zeta, dec, seq):
    n, nin = z.shape
    sg_heads = sgw.shape[0]
    ret_heads = dmask.shape[0]
    mix = sg_heads * SG_HEAD_DIM + ret_heads * RET_HEAD_DIM
    tc = min(TC_MIXER, seq)
    per_seq = seq // tc
    const3 = lambda i: (0, 0, 0)
    kern = functools.partial(_mixer_kernel, sg_heads=sg_heads, ret_heads=ret_heads,
                             steps_per_seq=per_seq)
    return pl.pallas_call(
        kern,
        out_shape=jax.ShapeDtypeStruct((n, mix), BF16),
        grid=(n // tc,),
        in_specs=[pl.BlockSpec((tc, nin), lambda i: (i, 0)),
                  pl.BlockSpec((tc, LANES), lambda i: (i % per_seq, 0)),
                  pl.BlockSpec((tc, LANES), lambda i: (i % per_seq, 0)),
                  pl.BlockSpec(sgw.shape, const3),
                  pl.BlockSpec(sgb.shape, const3),
                  pl.BlockSpec(dmask.shape, const3),
                  pl.BlockSpec(xi.shape, const3),
                  pl.BlockSpec(zeta.shape, const3),
                  pl.BlockSpec(dec.shape, const3)],
        out_specs=pl.BlockSpec((tc, mix), lambda i: (i, 0)),
        scratch_shapes=[pltpu.VMEM((sg_heads, CHUNK, CHUNK), BF16),
                        pltpu.VMEM((ret_heads, RET_HEAD_DIM, RET_HEAD_DIM), F32)],
        compiler_params=_params(1),
        name="mixer",
    )(z, cos, sin, sgw, sgb, dmask, xi, zeta, dec)


def _route_top2(logits):
    lane = lax.broadcasted_iota(jnp.int32, logits.shape, 1)
    m1 = jnp.max(logits, axis=-1, keepdims=True)
    i1 = jnp.min(jnp.where(logits == m1, lane, LANES), axis=-1, keepdims=True)
    rest = jnp.where(lane == i1, -jnp.inf, logits)
    m2 = jnp.max(rest, axis=-1, keepdims=True)
    i2 = jnp.min(jnp.where(rest == m2, lane, LANES), axis=-1, keepdims=True)
    e = jnp.exp(m2 - m1)
    den = 1.0 + e
    w1 = 1.0 / den
    w2 = e / den
    return jnp.where(lane == 0, i1.astype(F32),
                     jnp.where(lane == 1, i2.astype(F32),
                               jnp.where(lane == 2, w1, jnp.where(lane == 3, w2, 0.0))))


def _outproj_kernel(*refs, n_experts):
    if n_experts:
        y_ref, w_ref, x_ref, mod_ref, nw_ref, rw_ref, rb_ref, xo_ref, h2_ref, r_ref = refs
    else:
        y_ref, w_ref, x_ref, mod_ref, nw_ref, xo_ref, h2_ref = refs
    yo = jnp.dot(y_ref[...], w_ref[...], preferred_element_type=F32)
    xn = x_ref[...] + mod_ref[2:3, :] * yo
    xo_ref[...] = xn
    h2 = _rms_modulate(xn, nw_ref[...], mod_ref[3:4, :], mod_ref[4:5, :])
    h2_ref[...] = h2.astype(h2_ref.dtype)
    if n_experts:
        logits = jnp.dot(h2, rw_ref[...], preferred_element_type=F32,
                         precision=lax.Precision.HIGHEST) + rb_ref[...]
        lane = lax.broadcasted_iota(jnp.int32, logits.shape, 1)
        r_ref[...] = _route_top2(jnp.where(lane < n_experts, logits, -jnp.inf))


def _outproj(y, w, x, mod, nw, seq, router=None):
    n, d = x.shape
    mix = y.shape[1]
    tm = min(TM_OUTPROJ, seq)
    per_b = seq // tm
    row = lambda i: (i, 0)
    const = lambda i: (0, 0)
    in_specs = [pl.BlockSpec((tm, mix), row),
                pl.BlockSpec((mix, d), const),
                pl.BlockSpec((tm, d), row),
                pl.BlockSpec((None, 6, d), lambda i: (i // per_b, 0, 0)),
                pl.BlockSpec((1, d), const)]
    args = [y, w, x, mod, nw]
    out_shape = [jax.ShapeDtypeStruct((n, d), F32)]
    out_specs = [pl.BlockSpec((tm, d), row)]
    n_experts = 0
    if router is None:
        out_shape.append(jax.ShapeDtypeStruct((n, d), BF16))
        out_specs.append(pl.BlockSpec((tm, d), row))
    else:
        rw, rb = router
        n_experts = rw.shape[1]
        rw = jnp.pad(rw, ((0, 0), (0, LANES - n_experts)))
        rb = jnp.pad(rb, (0, LANES - n_experts)).reshape(1, LANES)
        in_specs += [pl.BlockSpec((d, LANES), const), pl.BlockSpec((1, LANES), const)]
        args += [rw, rb]
        out_shape += [jax.ShapeDtypeStruct((n, d), F32), jax.ShapeDtypeStruct((n, LANES), F32)]
        out_specs += [pl.BlockSpec((tm, d), row), pl.BlockSpec((tm, LANES), row)]
    return pl.pallas_call(
        functools.partial(_outproj_kernel, n_experts=n_experts),
        out_shape=out_shape,
        grid=(n // tm,),
        in_specs=in_specs,
        out_specs=out_specs,
        compiler_params=_params(1),
        name="outproj_router" if n_experts else "outproj",
    )(*args)


def _swiglu_accumulate(h_ref, wg_ref, wu_ref, wd_ref, acc_ref, first):
    h = h_ref[...]
    g = jnp.dot(h, wg_ref[...], preferred_element_type=F32)
    u = jnp.dot(h, wu_ref[...], preferred_element_type=F32)
    a = (_silu(g) * u).astype(BF16)
    part = jnp.dot(a, wd_ref[...], preferred_element_type=F32)

    @pl.when(first)
    def _():
        acc_ref[...] = part

    @pl.when(jnp.logical_not(first))
    def _():
        acc_ref[...] += part


def _ffn_dense_kernel(h_ref, wg_ref, wu_ref, wd_ref, x_ref, mod_ref, o_ref, acc_ref):
    j = pl.program_id(1)
    _swiglu_accumulate(h_ref, wg_ref, wu_ref, wd_ref, acc_ref, j == 0)

    @pl.when(j == pl.num_programs(1) - 1)
    def _():
        o_ref[...] = x_ref[...] + mod_ref[5:6, :] * acc_ref[...]


def _ffn_dense(h2, wg, wu, wd, x, mod, seq):
    n, d = x.shape
    f = wg.shape[1]
    tm = min(TM_FFN, seq)
    tf = min(TF_FFN, f)
    per_b = seq // tm
    return pl.pallas_call(
        _ffn_dense_kernel,
        out_shape=jax.ShapeDtypeStruct((n, d), F32),
        grid=(n // tm, f // tf),
        in_specs=[pl.BlockSpec((tm, d), lambda i, j: (i, 0)),
                  pl.BlockSpec((d, tf), lambda i, j: (0, j)),
                  pl.BlockSpec((d, tf), lambda i, j: (0, j)),
                  pl.BlockSpec((tf, d), lambda i, j: (j, 0)),
                  pl.BlockSpec((tm, d), lambda i, j: (i, 0)),
                  pl.BlockSpec((None, 6, d), lambda i, j: (i // per_b, 0, 0))],
        out_specs=pl.BlockSpec((tm, d), lambda i, j: (i, 0)),
        scratch_shapes=[pltpu.VMEM((tm, d), F32)],
        compiler_params=_params(2),
        name="ffn_dense",
    )(h2, wg, wu, wd, x, mod)


def _ffn_moe_kernel(te_ref, nu_ref, h_ref, rw_ref, wg_ref, wu_ref, wd_ref, o_ref, acc_ref):
    i = pl.program_id(0)
    j = pl.program_id(1)
    last = j == pl.num_programs(1) - 1

    @pl.when(i < nu_ref[0])
    def _():
        _swiglu_accumulate(h_ref, wg_ref, wu_ref, wd_ref, acc_ref, j == 0)

        @pl.when(last)
        def _():
            o_ref[...] = rw_ref[...] * acc_ref[...]

    @pl.when(jnp.logical_and(i >= nu_ref[0], last))
    def _():
        o_ref[...] = jnp.zeros_like(o_ref)


def _ffn_moe(tile_expert, n_used, hg, row_w, wg, wu, wd, tm):
    r, d = hg.shape
    f = wg.shape[2]
    tf = min(TF_FFN, f)
    nj = f // tf

    def tile(i, nu):
        return jnp.minimum(i, nu[0] - 1)

    def feat(i, j, nu):
        return jnp.where(i < nu[0], j, nj - 1)

    grid_spec = pltpu.PrefetchScalarGridSpec(
        num_scalar_prefetch=2,
        grid=(r // tm, nj),
        in_specs=[pl.BlockSpec((tm, d), lambda i, j, te, nu: (tile(i, nu), 0)),
                  pl.BlockSpec((tm, 1), lambda i, j, te, nu: (tile(i, nu), 0)),
                  pl.BlockSpec((None, d, tf), lambda i, j, te, nu: (te[i], 0, feat(i, j, nu))),
                  pl.BlockSpec((None, d, tf), lambda i, j, te, nu: (te[i], 0, feat(i, j, nu))),
                  pl.BlockSpec((None, tf, d), lambda i, j, te, nu: (te[i], feat(i, j, nu), 0))],
        out_specs=pl.BlockSpec((tm, d), lambda i, j, te, nu: (i, 0)),
        scratch_shapes=[pltpu.VMEM((tm, d), F32)],
    )
    return pl.pallas_call(
        _ffn_moe_kernel,
        out_shape=jax.ShapeDtypeStruct((r, d), F32),
        grid_spec=grid_spec,
        compiler_params=_params(2),
        name="ffn_moe",
    )(tile_expert, n_used, hg, row_w, wg, wu, wd)


def _row_copy(src_hbm, dst_ref, src_row, dst_row, sem):
    return pltpu.make_async_copy(src_hbm.at[pl.ds(src_row, 1), :],
                                 dst_ref.at[pl.ds(dst_row, 1), :], sem)


def _gather_rows_kernel(idx_ref, src_hbm, o_ref, buf_ref, sem):
    rows = buf_ref.shape[0]

    def issue(k, carry):
        _row_copy(src_hbm, buf_ref, idx_ref[0, k], k, sem).start()
        return carry

    lax.fori_loop(0, rows, issue, 0)
    pltpu.make_async_copy(src_hbm.at[pl.ds(0, rows), :], buf_ref, sem).wait()
    o_ref[...] = buf_ref[...].astype(o_ref.dtype)


def _gather_rows(src, idx, tg):
    r = idx.shape[0]
    d = src.shape[1]
    return pl.pallas_call(
        _gather_rows_kernel,
        out_shape=jax.ShapeDtypeStruct((r, d), BF16),
        grid=(r // tg,),
        in_specs=[pl.BlockSpec((None, 1, tg), lambda i: (i, 0, 0), memory_space=pltpu.SMEM),
                  pl.BlockSpec(memory_space=pl.ANY)],
        out_specs=pl.BlockSpec((tg, d), lambda i: (i, 0)),
        scratch_shapes=[pltpu.VMEM((tg, d), src.dtype), pltpu.SemaphoreType.DMA],
        compiler_params=_params(1),
        name="gather_rows",
    )(idx.reshape(r // tg, 1, tg), src)


def _combine_kernel(p0_ref, p1_ref, o_hbm, x_ref, mod_ref, out_ref, buf0_ref, buf1_ref, sem):
    rows = buf0_ref.shape[0]

    def issue(k, carry):
        _row_copy(o_hbm, buf0_ref, p0_ref[0, k], k, sem).start()
        _row_copy(o_hbm, buf1_ref, p1_ref[0, k], k, sem).start()
        return carry

    lax.fori_loop(0, rows, issue, 0)
    pltpu.make_async_copy(o_hbm.at[pl.ds(0, rows), :], buf0_ref, sem).wait()
    pltpu.make_async_copy(o_hbm.at[pl.ds(0, rows), :], buf1_ref, sem).wait()
    out_ref[...] = x_ref[...] + mod_ref[5:6, :] * (buf0_ref[...] + buf1_ref[...])


def _combine(o, p0, p1, x, mod, seq):
    n, d = x.shape
    tr = min(TR_COMBINE, seq)
    per_b = seq // tr
    idx_spec = pl.BlockSpec((None, 1, tr), lambda i: (i, 0, 0), memory_space=pltpu.SMEM)
    return pl.pallas_call(
        _combine_kernel,
        out_shape=jax.ShapeDtypeStruct((n, d), F32),
        grid=(n // tr,),
        in_specs=[idx_spec, idx_spec,
                  pl.BlockSpec(memory_space=pl.ANY),
                  pl.BlockSpec((tr, d), lambda i: (i, 0)),
                  pl.BlockSpec((None, 6, d), lambda i: (i // per_b, 0, 0))],
        out_specs=pl.BlockSpec((tr, d), lambda i: (i, 0)),
        scratch_shapes=[pltpu.VMEM((tr, d), F32), pltpu.VMEM((tr, d), F32),
                        pltpu.SemaphoreType.DMA],
        compiler_params=_params(1),
        name="moe_combine",
    )(p0.reshape(n // tr, 1, tr), p1.reshape(n // tr, 1, tr), o, x, mod)


def _routing_tables(route, n_experts, tm):
    n = route.shape[0]
    ids = route[:, :TOP_K].astype(jnp.int32).reshape(-1)
    wts = route[:, TOP_K:2 * TOP_K].reshape(-1)
    n_assign = n * TOP_K
    n_tiles = n_assign // tm + n_experts
    onehot = (ids[:, None] == jnp.arange(n_experts, dtype=jnp.int32)[None, :]).astype(jnp.int32)
    ranks = jnp.cumsum(onehot, axis=0)
    counts = ranks[-1]
    rank = jnp.sum((ranks - 1) * onehot, axis=1)
    offs = jnp.cumsum(counts) - counts
    tiles_per = (counts + tm - 1) // tm
    tile_end = jnp.cumsum(tiles_per)
    n_used = tile_end[-1]
    pad_offs = (tile_end - tiles_per) * tm
    pos = pad_offs[ids] + rank
    tile_ids = jnp.arange(n_tiles, dtype=jnp.int32)
    tile_expert = jnp.sum((tile_ids[:, None] >= tile_end[None, :]).astype(jnp.int32), axis=1)
    last_expert = jnp.sum((n_used - 1 >= tile_end).astype(jnp.int32))
    tile_expert = jnp.where(tile_ids < n_used, tile_expert, last_expert).astype(jnp.int32)
    order = jnp.argsort(ids, stable=True).astype(jnp.int32)
    rows = jnp.arange(n_tiles * tm, dtype=jnp.int32)
    row_expert = tile_expert[rows // tm]
    within = rows - pad_offs[row_expert]
    valid = (within < counts[row_expert]) & (rows // tm < n_used)
    src_assign = order[jnp.clip(offs[row_expert] + within, 0, n_assign - 1)]
    src_token = jnp.where(valid, src_assign // TOP_K, 0).astype(jnp.int32)
    row_w = jnp.where(valid, wts[src_assign], 0.0).reshape(-1, 1)
    pos = pos.reshape(n, TOP_K).astype(jnp.int32)
    return tile_expert, n_used.reshape(1).astype(jnp.int32), src_token, row_w, pos[:, 0], pos[:, 1]


def _final_norm_kernel(x_ref, w_ref, o_ref):
    x = x_ref[...]
    o_ref[...] = (x * lax.rsqrt(jnp.mean(x * x, axis=-1, keepdims=True) + EPS)) * w_ref[...]


def _final_norm(x, w):
    n, d = x.shape
    tm = min(TM_NORM, n)
    return pl.pallas_call(
        _final_norm_kernel,
        out_shape=jax.ShapeDtypeStruct((n, d), F32),
        grid=(n // tm,),
        in_specs=[pl.BlockSpec((tm, d), lambda i: (i, 0)), pl.BlockSpec((1, d), lambda i: (0, 0))],
        out_specs=pl.BlockSpec((tm, d), lambda i: (i, 0)),
        compiler_params=_params(1),
        name="final_norm",
    )(x, w)


def _retention_tables(ret_heads, seq):
    half = RET_HEAD_DIM // 2
    inv_freq = 1.0 / (ROPE_BASE ** (jnp.arange(half, dtype=F32) * 2.0 / RET_HEAD_DIM))
    ang = jnp.arange(seq, dtype=F32)[:, None] * inv_freq[None, :]
    log_gamma = jnp.log(1.0 - 2.0 ** (-5.0 - jnp.arange(ret_heads, dtype=F32)))
    j = jnp.arange(CHUNK, dtype=F32)
    diff = j[:, None] - j[None, :]
    dmask = jnp.where(diff >= 0, jnp.exp(log_gamma[:, None, None] * jnp.maximum(diff, 0.0)), 0.0)
    xi = jnp.exp(log_gamma[:, None] * (j[None, :] + 1.0))
    zeta = jnp.exp(log_gamma[:, None] * (CHUNK - 1.0 - j[None, :]))
    dec = jnp.exp(log_gamma * CHUNK)
    wide = (ret_heads, CHUNK, RET_HEAD_DIM)
    return (jnp.cos(ang), jnp.sin(ang), dmask,
            jnp.broadcast_to(xi[:, :, None], wide), jnp.broadcast_to(zeta[:, :, None], wide),
            jnp.broadcast_to(dec[:, None, None], (ret_heads, 1, RET_HEAD_DIM)))


def kernel(x, c, norm1_w, ada_w, ada_b, w_in, sg_w, sg_b, w_out, norm2_w, ffn_w_gate, ffn_w_up, ffn_w_down, router_w, router_b, moe_w_gate, moe_w_up, moe_w_down, final_norm_w):
    nb, seq, d = x.shape
    depth = w_in.shape[0]
    sg_heads = sg_w.shape[1]
    ret_heads = (w_in.shape[2] - 2 * sg_heads * SG_HEAD_DIM) // (4 * RET_HEAD_DIM)
    n = nb * seq
    assert seq % CHUNK == 0 and sg_w.shape[2] == CHUNK

    mod = _modulation(c, ada_w, ada_b).reshape(depth, nb, 6, d)
    cos, sin, dmask, xi, zeta, dec = _retention_tables(ret_heads, seq)
    xf = x.reshape(n, d)
    for l in range(depth):
        z = _inproj(xf, norm1_w[l].reshape(1, d), mod[l], w_in[l].astype(BF16), seq)
        sgb = jnp.broadcast_to(sg_b[l][:, :, None], (sg_heads, CHUNK, LANES))
        y = _mixer(z, cos, sin, sg_w[l], sgb, dmask, xi, zeta, dec, seq)
        i = l // 2
        nw2 = norm2_w[l].reshape(1, d)
        if l % 2 == 0:
            xf, h2 = _outproj(y, w_out[l].astype(BF16), xf, mod[l], nw2, seq)
            xf = _ffn_dense(h2, ffn_w_gate[i].astype(BF16), ffn_w_up[i].astype(BF16),
                            ffn_w_down[i].astype(BF16), xf, mod[l], seq)
        else:
            n_experts = router_w.shape[2]
            tm = min(TM_MOE, n)
            xf, h2, route = _outproj(y, w_out[l].astype(BF16), xf, mod[l], nw2, seq,
                                     router=(router_w[i], router_b[i]))
            tile_expert, n_used, src_token, row_w, p0, p1 = _routing_tables(route, n_experts, tm)
            hg = _gather_rows(h2, src_token, min(TG_GATHER, tm))
            o = _ffn_moe(tile_expert, n_used, hg, row_w, moe_w_gate[i].astype(BF16),
                         moe_w_up[i].astype(BF16), moe_w_down[i].astype(BF16), tm)
            xf = _combine(o, p0, p1, xf, mod[l], seq)
    return _final_norm(xf, final_norm_w.reshape(1, d)).reshape(nb, seq, d)
```

```python
import functools

import jax
import jax.numpy as jnp
from jax import lax
from jax.experimental import pallas as pl
from jax.experimental.pallas import tpu as pltpu

F32 = jnp.float32
BF16 = jnp.bfloat16

EPS = 1e-6
ROPE_BASE = 10000.0
SG_HEAD_DIM = 128
RET_HEAD_DIM = 256
CHUNK = 128
TOP_K = 2
LANES = 128

TM_INPROJ = 1024
TN_INPROJ = 1024
TC_MIXER = 512
TM_OUTPROJ = 512
ROW_CHUNKS_OUTPROJ = 2
TM_FFN = 1024
TF_FFN = 512
TN_DOWN_DENSE = 512
TN_DOWN_MOE = 256
TG_GATHER = 1024
TR_COMBINE = 512
TN_MOD = 512
DMA_ISSUE_UNROLL = 8
VMEM_LIMIT = 60 * 1024 * 1024


def _params(n_axes):
    return pltpu.CompilerParams(dimension_semantics=("arbitrary",) * n_axes,
                                vmem_limit_bytes=VMEM_LIMIT)


def _gelu_tanh(x):
    return 0.5 * x * (1.0 + jnp.tanh(0.7978845608028654 * (x + 0.044715 * (x * x * x))))


def _silu(x):
    return x * jax.nn.sigmoid(x)


def _rms_modulate(x, nw, shift, scale):
    y = x * lax.rsqrt(jnp.mean(x * x, axis=-1, keepdims=True) + EPS)
    return (y * nw) * (1.0 + scale) + shift


def _mod_kernel(cb_ref, w_ref, b_ref, o_ref):
    tn = w_ref.shape[1]
    for b in range(cb_ref.shape[0]):
        act = _silu(cb_ref[b])
        cols = [jnp.sum(w_ref[:, j * LANES:(j + 1) * LANES] * act, axis=0, keepdims=True)
                for j in range(tn // LANES)]
        o_ref[b:b + 1, :] = jnp.concatenate(cols, axis=1) + b_ref[...]


def _modulation(c, ada_w, ada_b):
    depth, d, n6 = ada_w.shape
    nb = c.shape[0]
    tn = min(TN_MOD, n6)
    cb = jnp.broadcast_to(c[:, :, None], (nb, d, LANES))
    return pl.pallas_call(
        _mod_kernel,
        out_shape=jax.ShapeDtypeStruct((depth, nb, n6), F32),
        grid=(depth, n6 // tn),
        in_specs=[pl.BlockSpec((nb, d, LANES), lambda l, j: (0, 0, 0)),
                  pl.BlockSpec((None, d, tn), lambda l, j: (l, 0, j)),
                  pl.BlockSpec((None, 1, tn), lambda l, j: (l, 0, j))],
        out_specs=pl.BlockSpec((None, nb, tn), lambda l, j: (l, 0, j)),
        compiler_params=_params(2),
        name="adaln_mod",
    )(cb, ada_w, ada_b.reshape(depth, 1, n6))


def _inproj_kernel(x_ref, nw_ref, mod_ref, w_ref, z_ref, h_ref):
    @pl.when(pl.program_id(1) == 0)
    def _():
        h = _rms_modulate(x_ref[...], nw_ref[...], mod_ref[0:1, :], mod_ref[1:2, :])
        h_ref[...] = h.astype(BF16)

    z_ref[...] = jnp.dot(h_ref[...], w_ref[...], preferred_element_type=F32).astype(z_ref.dtype)


def _inproj(x, nw, mod, w, layer, seq):
    n, d = x.shape
    nin = w.shape[2]
    tm = min(TM_INPROJ, seq)
    tn = min(TN_INPROJ, nin)
    per_b = seq // tm
    return pl.pallas_call(
        _inproj_kernel,
        out_shape=jax.ShapeDtypeStruct((n, nin), BF16),
        grid=(n // tm, nin // tn),
        in_specs=[pl.BlockSpec((tm, d), lambda i, j: (i, 0)),
                  pl.BlockSpec((1, d), lambda i, j: (0, 0)),
                  pl.BlockSpec((None, 6, d), lambda i, j: (i // per_b, 0, 0)),
                  pl.BlockSpec((None, d, tn), lambda i, j: (layer, 0, j))],
        out_specs=pl.BlockSpec((tm, tn), lambda i, j: (i, j)),
        scratch_shapes=[pltpu.VMEM((tm, d), BF16)],
        compiler_params=_params(2),
        name="inproj",
    )(x, nw, mod, w)


def _mixer_kernel(z_ref, cos_ref, sin_ref, sgw_ref, sgb_ref, dmask_ref, xi_ref, zeta_ref,
                  dec_ref, y_ref, wm_ref, state_ref, *, sg_heads, ret_heads, steps_per_seq):
    sgw_w = sg_heads * SG_HEAD_DIM
    ret_w = ret_heads * RET_HEAD_DIM
    q_off, k_off = 2 * sgw_w, 2 * sgw_w + ret_w
    v_off, g_off = 2 * sgw_w + 2 * ret_w, 2 * sgw_w + 3 * ret_w
    half = RET_HEAD_DIM // 2
    step = pl.program_id(0)

    @pl.when(step == 0)
    def _():
        t = lax.broadcasted_iota(jnp.int32, (CHUNK, CHUNK), 0)
        s = lax.broadcasted_iota(jnp.int32, (CHUNK, CHUNK), 1)
        causal = (t >= s).astype(F32)
        for h in range(sg_heads):
            wm_ref[h] = (sgw_ref[h] * causal).astype(BF16)

    @pl.when(step % steps_per_seq == 0)
    def _():
        state_ref[...] = jnp.zeros_like(state_ref)

    def rotate(t, cos, sin):
        t1, t2 = t[:, :half], t[:, half:]
        return jnp.concatenate([t1 * cos - t2 * sin, t1 * sin + t2 * cos], axis=-1)

    def chunk(c, carry):
        rows = pl.ds(pl.multiple_of(c * CHUNK, CHUNK), CHUNK)
        cos = cos_ref[rows, :]
        sin = sin_ref[rows, :]
        for h in range(sg_heads):
            lo = h * SG_HEAD_DIM
            u = _gelu_tanh(z_ref[rows, lo:lo + SG_HEAD_DIM].astype(F32))
            v = _gelu_tanh(z_ref[rows, sgw_w + lo:sgw_w + lo + SG_HEAD_DIM].astype(F32))
            vc = v - jnp.mean(v, axis=-1, keepdims=True)
            vn = vc * lax.rsqrt(jnp.mean(vc * vc, axis=-1, keepdims=True) + EPS)
            sv = jnp.dot(wm_ref[h], vn.astype(BF16), preferred_element_type=F32) + sgb_ref[h]
            y_ref[rows, lo:lo + SG_HEAD_DIM] = (u * sv).astype(y_ref.dtype)
        for h in range(ret_heads):
            lo = h * RET_HEAD_DIM
            q = z_ref[rows, q_off + lo:q_off + lo + RET_HEAD_DIM].astype(F32)
            k = z_ref[rows, k_off + lo:k_off + lo + RET_HEAD_DIM].astype(F32)
            v = z_ref[rows, v_off + lo:v_off + lo + RET_HEAD_DIM]
            g = z_ref[rows, g_off + lo:g_off + lo + RET_HEAD_DIM].astype(F32)
            qr = rotate(q, cos, sin)
            kr = rotate(k, cos, sin) * (RET_HEAD_DIM ** -0.5)
            qb = qr.astype(BF16)
            scores = lax.dot_general(qb, kr.astype(BF16), (((1,), (1,)), ((), ())),
                                     preferred_element_type=F32) * dmask_ref[h]
            st = state_ref[h]
            yv = (jnp.dot(scores.astype(BF16), v, preferred_element_type=F32)
                  + jnp.dot(qb, st.astype(BF16), preferred_element_type=F32) * xi_ref[h])
            kz = (kr * zeta_ref[h]).astype(BF16)
            state_ref[h] = st * dec_ref[h] + lax.dot_general(
                kz, v, (((0,), (0,)), ((), ())), preferred_element_type=F32)
            yn = yv * lax.rsqrt(jnp.mean(yv * yv, axis=-1, keepdims=True) + EPS)
            y_ref[rows, sgw_w + lo:sgw_w + lo + RET_HEAD_DIM] = (_silu(g) * yn).astype(y_ref.dtype)
        return carry

    lax.fori_loop(0, z_ref.shape[0] // CHUNK, chunk, 0)


def _mixer(z, cos, sin, sgw, sgb, dmask, xi, zeta, dec, seq):
    n, nin = z.shape
    sg_heads = sgw.shape[0]
    ret_heads = dmask.shape[0]
    mix = sg_heads * SG_HEAD_DIM + ret_heads * RET_HEAD_DIM
    tc = min(TC_MIXER, seq)
    per_seq = seq // tc
    const3 = lambda i: (0, 0, 0)
    kern = functools.partial(_mixer_kernel, sg_heads=sg_heads, ret_heads=ret_heads,
                             steps_per_seq=per_seq)
    return pl.pallas_call(
        kern,
        out_shape=jax.ShapeDtypeStruct((n, mix), BF16),
        grid=(n // tc,),
        in_specs=[pl.BlockSpec((tc, nin), lambda i: (i, 0)),
                  pl.BlockSpec((tc, LANES), lambda i: (i % per_seq, 0)),
                  pl.BlockSpec((tc, LANES), lambda i: (i % per_seq, 0)),
                  pl.BlockSpec(sgw.shape, const3),
                  pl.BlockSpec(sgb.shape, const3),
                  pl.BlockSpec(dmask.shape, const3),
                  pl.BlockSpec(xi.shape, const3),
                  pl.BlockSpec(zeta.shape, const3),
                  pl.BlockSpec(dec.shape, const3)],
        out_specs=pl.BlockSpec((tc, mix), lambda i: (i, 0)),
        scratch_shapes=[pltpu.VMEM((sg_heads, CHUNK, CHUNK), BF16),
                        pltpu.VMEM((ret_heads, RET_HEAD_DIM, RET_HEAD_DIM), F32)],
        compiler_params=_params(1),
        name="mixer",
    )(z, cos, sin, sgw, sgb, dmask, xi, zeta, dec)


def _route_top2(logits):
    lane = lax.broadcasted_iota(jnp.int32, logits.shape, 1)
    m1 = jnp.max(logits, axis=-1, keepdims=True)
    i1 = jnp.min(jnp.where(logits == m1, lane, LANES), axis=-1, keepdims=True)
    rest = jnp.where(lane == i1, -jnp.inf, logits)
    m2 = jnp.max(rest, axis=-1, keepdims=True)
    i2 = jnp.min(jnp.where(rest == m2, lane, LANES), axis=-1, keepdims=True)
    e = jnp.exp(m2 - m1)
    den = 1.0 + e
    w1 = 1.0 / den
    w2 = e / den
    return jnp.where(lane == 0, i1.astype(F32),
                     jnp.where(lane == 1, i2.astype(F32),
                               jnp.where(lane == 2, w1, jnp.where(lane == 3, w2, 0.0))))


def _router_logits(h2, rwt_ref, rb_ref, n_experts):
    lane = lax.broadcasted_iota(jnp.int32, (h2.shape[0], LANES), 1)
    logits = jnp.full((h2.shape[0], LANES), -jnp.inf, F32)
    for e in range(n_experts):
        s = jnp.sum(h2 * rwt_ref[e:e + 1, :], axis=-1, keepdims=True)
        logits = jnp.where(lane == e, s, logits)
    return logits + rb_ref[...]


def _outproj_kernel(*refs, n_experts, row_chunks):
    if n_experts:
        y_ref, w_ref, x_ref, mod_ref, nw_ref, rwt_ref, rb_ref, xo_ref, h2_ref, r_ref = refs
    else:
        y_ref, w_ref, x_ref, mod_ref, nw_ref, xo_ref, h2_ref = refs
    rc = y_ref.shape[0] // row_chunks
    for c in range(row_chunks):
        rows = slice(c * rc, (c + 1) * rc)
        yo = jnp.dot(y_ref[rows, :], w_ref[...], preferred_element_type=F32)
        xn = x_ref[rows, :] + mod_ref[2:3, :] * yo
        xo_ref[rows, :] = xn
        h2 = _rms_modulate(xn, nw_ref[...], mod_ref[3:4, :], mod_ref[4:5, :])
        h2_ref[rows, :] = h2.astype(h2_ref.dtype)
        if n_experts:
            r_ref[rows, :] = _route_top2(_router_logits(h2, rwt_ref, rb_ref, n_experts))


def _outproj(y, w, layer, x, mod, nw, seq, router=None):
    n, d = x.shape
    mix = y.shape[1]
    tm = min(TM_OUTPROJ, seq)
    per_b = seq // tm
    row = lambda i: (i, 0)
    const = lambda i: (0, 0)
    in_specs = [pl.BlockSpec((tm, mix), row),
                pl.BlockSpec((None, mix, d), lambda i: (layer, 0, 0)),
                pl.BlockSpec((tm, d), row),
                pl.BlockSpec((None, 6, d), lambda i: (i // per_b, 0, 0)),
                pl.BlockSpec((1, d), const)]
    args = [y, w, x, mod, nw]
    out_shape = [jax.ShapeDtypeStruct((n, d), F32)]
    out_specs = [pl.BlockSpec((tm, d), row)]
    n_experts = 0
    if router is None:
        out_shape.append(jax.ShapeDtypeStruct((n, d), BF16))
        out_specs.append(pl.BlockSpec((tm, d), row))
    else:
        rw, rb = router
        n_experts = rw.shape[1]
        rb = jnp.pad(rb, (0, LANES - n_experts)).reshape(1, LANES)
        in_specs += [pl.BlockSpec((n_experts, d), const), pl.BlockSpec((1, LANES), const)]
        args += [rw.T, rb]
        out_shape += [jax.ShapeDtypeStruct((n, d), F32), jax.ShapeDtypeStruct((n, LANES), F32)]
        out_specs += [pl.BlockSpec((tm, d), row), pl.BlockSpec((tm, LANES), row)]
    return pl.pallas_call(
        functools.partial(_outproj_kernel, n_experts=n_experts, row_chunks=ROW_CHUNKS_OUTPROJ),
        out_shape=out_shape,
        grid=(n // tm,),
        in_specs=in_specs,
        out_specs=out_specs,
        compiler_params=_params(1),
        name="outproj_router" if n_experts else "outproj",
    )(*args)


def _ffn_up_step(h_ref, wg_ref, wu_ref, a_ref, j, nj, tf):
    h = h_ref[...]
    g = jnp.dot(h, wg_ref[...].astype(BF16), preferred_element_type=F32)
    u = jnp.dot(h, wu_ref[...].astype(BF16), preferred_element_type=F32)
    a = (_silu(g) * u).astype(BF16)
    for c in range(nj):
        @pl.when(j == c)
        def _():
            a_ref[:, c * tf:(c + 1) * tf] = a


def _ffn_down(a_ref, wd_ref):
    return jnp.dot(a_ref[...], wd_ref[...].astype(BF16), preferred_element_type=F32)


def _ffn_dense_kernel(h_ref, wg_ref, wu_ref, wd_ref, x_ref, mod_ref, o_ref, a_ref, *, nj, tf):
    j = pl.program_id(1)

    @pl.when(j < nj)
    def _():
        _ffn_up_step(h_ref, wg_ref, wu_ref, a_ref, j, nj, tf)

    @pl.when(j >= nj)
    def _():
        o_ref[...] = x_ref[...] + mod_ref[5:6, :] * _ffn_down(a_ref, wd_ref)


def _ffn_dense(h2, wg, wu, wd, layer, x, mod, seq):
    n, d = x.shape
    f = wg.shape[2]
    tm = min(TM_FFN, seq)
    tf = min(TF_FFN, f)
    tn = min(TN_DOWN_DENSE, d)
    nj, nd = f // tf, d // tn
    per_b = seq // tm
    feat = lambda j: jnp.minimum(j, nj - 1)
    col = lambda j: jnp.maximum(j - nj, 0)
    return pl.pallas_call(
        functools.partial(_ffn_dense_kernel, nj=nj, tf=tf),
        out_shape=jax.ShapeDtypeStruct((n, d), F32),
        grid=(n // tm, nj + nd),
        in_specs=[pl.BlockSpec((tm, d), lambda i, j: (i, 0)),
                  pl.BlockSpec((None, d, tf), lambda i, j: (layer, 0, feat(j))),
                  pl.BlockSpec((None, d, tf), lambda i, j: (layer, 0, feat(j))),
                  pl.BlockSpec((None, f, tn), lambda i, j: (layer, 0, col(j))),
                  pl.BlockSpec((tm, tn), lambda i, j: (i, col(j))),
                  pl.BlockSpec((None, 6, tn), lambda i, j: (i // per_b, 0, col(j)))],
        out_specs=pl.BlockSpec((tm, tn), lambda i, j: (i, col(j))),
        scratch_shapes=[pltpu.VMEM((tm, f), BF16)],
        compiler_params=_params(2),
        name="ffn_dense",
    )(h2, wg, wu, wd, x, mod)


def _ffn_moe_kernel(te_ref, nu_ref, h_ref, wg_ref, wu_ref, wd_ref, o_ref, a_ref, *, nj, tf):
    j = pl.program_id(1)
    active = pl.program_id(0) < nu_ref[0]

    @pl.when(jnp.logical_and(active, j < nj))
    def _():
        _ffn_up_step(h_ref, wg_ref, wu_ref, a_ref, j, nj, tf)

    @pl.when(jnp.logical_and(active, j >= nj))
    def _():
        o_ref[...] = _ffn_down(a_ref, wd_ref)

    @pl.when(jnp.logical_and(jnp.logical_not(active), j >= nj))
    def _():
        o_ref[...] = jnp.zeros_like(o_ref)


def _ffn_moe(tile_expert, n_used, hg, wg, wu, wd, layer, tm):
    r, d = hg.shape
    f = wg.shape[3]
    tf = min(TF_FFN, f)
    tn = min(TN_DOWN_MOE, d)
    nj, nd = f // tf, d // tn

    def tile(i, nu):
        return jnp.minimum(i, nu[0] - 1)

    def feat(i, j, nu):
        return jnp.where(i < nu[0], jnp.minimum(j, nj - 1), nj - 1)

    def col_in(i, j, nu):
        return jnp.where(i < nu[0], jnp.maximum(j - nj, 0), nd - 1)

    grid_spec = pltpu.PrefetchScalarGridSpec(
        num_scalar_prefetch=2,
        grid=(r // tm, nj + nd),
        in_specs=[pl.BlockSpec((tm, d), lambda i, j, te, nu: (tile(i, nu), 0)),
                  pl.BlockSpec((None, None, d, tf),
                               lambda i, j, te, nu: (layer, te[i], 0, feat(i, j, nu))),
                  pl.BlockSpec((None, None, d, tf),
                               lambda i, j, te, nu: (layer, te[i], 0, feat(i, j, nu))),
                  pl.BlockSpec((None, None, f, tn),
                               lambda i, j, te, nu: (layer, te[i], 0, col_in(i, j, nu)))],
        out_specs=pl.BlockSpec((tm, tn), lambda i, j, te, nu: (i, jnp.maximum(j - nj, 0))),
        scratch_shapes=[pltpu.VMEM((tm, f), BF16)],
    )
    return pl.pallas_call(
        functools.partial(_ffn_moe_kernel, nj=nj, tf=tf),
        out_shape=jax.ShapeDtypeStruct((r, d), F32),
        grid_spec=grid_spec,
        compiler_params=_params(2),
        name="ffn_moe",
    )(tile_expert, n_used, hg, wg, wu, wd)


def _row_copy(src_hbm, dst_ref, src_row, dst_row, sem):
    return pltpu.make_async_copy(src_hbm.at[pl.ds(src_row, 1), :],
                                 dst_ref.at[pl.ds(dst_row, 1), :], sem)


def _gather_rows_kernel(idx_ref, src_hbm, o_ref, buf_ref, sem):
    rows = buf_ref.shape[0]

    def issue(k, carry):
        _row_copy(src_hbm, buf_ref, idx_ref[0, k], k, sem).start()
        return carry

    lax.fori_loop(0, rows, issue, 0, unroll=DMA_ISSUE_UNROLL)
    pltpu.make_async_copy(src_hbm.at[pl.ds(0, rows), :], buf_ref, sem).wait()
    o_ref[...] = buf_ref[...].astype(o_ref.dtype)


def _gather_rows(src, idx, tg):
    r = idx.shape[0]
    d = src.shape[1]
    return pl.pallas_call(
        _gather_rows_kernel,
        out_shape=jax.ShapeDtypeStruct((r, d), BF16),
        grid=(r // tg,),
        in_specs=[pl.BlockSpec((None, 1, tg), lambda i: (i, 0, 0), memory_space=pltpu.SMEM),
                  pl.BlockSpec(memory_space=pl.ANY)],
        out_specs=pl.BlockSpec((tg, d), lambda i: (i, 0)),
        scratch_shapes=[pltpu.VMEM((tg, d), src.dtype), pltpu.SemaphoreType.DMA],
        compiler_params=_params(1),
        name="gather_rows",
    )(idx.reshape(r // tg, 1, tg), src)


def _combine_kernel(*refs, final_norm):
    if final_norm:
        p0_ref, p1_ref, o_hbm, route_ref, x_ref, mod_ref, fw_ref, out_ref, buf0_ref, buf1_ref, sem = refs
    else:
        p0_ref, p1_ref, o_hbm, route_ref, x_ref, mod_ref, out_ref, buf0_ref, buf1_ref, sem = refs
    rows = buf0_ref.shape[0]

    def issue(k, carry):
        _row_copy(o_hbm, buf0_ref, p0_ref[0, k], k, sem).start()
        _row_copy(o_hbm, buf1_ref, p1_ref[0, k], k, sem).start()
        return carry

    lax.fori_loop(0, rows, issue, 0, unroll=DMA_ISSUE_UNROLL)
    pltpu.make_async_copy(o_hbm.at[pl.ds(0, rows), :], buf0_ref, sem).wait()
    pltpu.make_async_copy(o_hbm.at[pl.ds(0, rows), :], buf1_ref, sem).wait()
    w0 = route_ref[:, TOP_K:TOP_K + 1]
    w1 = route_ref[:, TOP_K + 1:TOP_K + 2]
    xn = x_ref[...] + mod_ref[5:6, :] * (w0 * buf0_ref[...] + w1 * buf1_ref[...])
    if final_norm:
        xn = (xn * lax.rsqrt(jnp.mean(xn * xn, axis=-1, keepdims=True) + EPS)) * fw_ref[...]
    out_ref[...] = xn


def _combine(o, p0, p1, route, x, mod, seq, final_w=None):
    n, d = x.shape
    tr = min(TR_COMBINE, seq)
    per_b = seq // tr
    idx_spec = pl.BlockSpec((None, 1, tr), lambda i: (i, 0, 0), memory_space=pltpu.SMEM)
    in_specs = [idx_spec, idx_spec,
                pl.BlockSpec(memory_space=pl.ANY),
                pl.BlockSpec((tr, LANES), lambda i: (i, 0)),
                pl.BlockSpec((tr, d), lambda i: (i, 0)),
                pl.BlockSpec((None, 6, d), lambda i: (i // per_b, 0, 0))]
    args = [p0.reshape(n // tr, 1, tr), p1.reshape(n // tr, 1, tr), o, route, x, mod]
    if final_w is not None:
        in_specs.append(pl.BlockSpec((1, d), lambda i: (0, 0)))
        args.append(final_w)
    return pl.pallas_call(
        functools.partial(_combine_kernel, final_norm=final_w is not None),
        out_shape=jax.ShapeDtypeStruct((n, d), F32),
        grid=(n // tr,),
        in_specs=in_specs,
        out_specs=pl.BlockSpec((tr, d), lambda i: (i, 0)),
        scratch_shapes=[pltpu.VMEM((tr, d), F32), pltpu.VMEM((tr, d), F32),
                        pltpu.SemaphoreType.DMA],
        compiler_params=_params(1),
        name="moe_combine",
    )(*args)


def _routing_tables(route, n_experts, tm):
    n = route.shape[0]
    i32 = jnp.int32
    ids = route[:, :TOP_K].astype(i32).reshape(-1)
    n_assign = n * TOP_K
    n_tiles = n_assign // tm + n_experts
    experts = jnp.arange(n_experts, dtype=i32)
    onehot = (ids[:, None] == experts[None, :]).astype(i32)
    ranks = jnp.cumsum(onehot, axis=0)
    counts = ranks[-1]
    offs = jnp.cumsum(counts) - counts
    tiles_per = (counts + tm - 1) // tm
    tile_end = jnp.cumsum(tiles_per)
    n_used = tile_end[-1]
    pad_offs = (tile_end - tiles_per) * tm
    pos = jnp.sum(onehot * (pad_offs[None, :] + ranks - 1), axis=1).reshape(n, TOP_K)
    tile_ids = jnp.arange(n_tiles, dtype=i32)
    tile_expert = jnp.sum((tile_ids[:, None] >= tile_end[None, :]).astype(i32), axis=1)
    last_expert = jnp.sum((n_used - 1 >= tile_end).astype(i32))
    tile_expert = jnp.where(tile_ids < n_used, tile_expert, last_expert).astype(i32)
    _, sorted_tok = lax.sort((ids, jnp.arange(n_assign, dtype=i32) // TOP_K), num_keys=1,
                             is_stable=True)
    row_expert = jnp.repeat(tile_expert, tm)
    rows = jnp.arange(n_tiles * tm, dtype=i32)
    shift = jnp.sum((row_expert[:, None] == experts[None, :]) * (offs - pad_offs)[None, :], axis=1)
    limit = jnp.sum((row_expert[:, None] == experts[None, :]) * (offs + counts)[None, :], axis=1)
    src = rows + shift
    valid = (src < limit) & (rows < n_used * tm)
    src_token = jnp.where(valid, sorted_tok[jnp.clip(src, 0, n_assign - 1)], 0).astype(i32)
    return (tile_expert, n_used.reshape(1).astype(i32), src_token,
            pos[:, 0].astype(i32), pos[:, 1].astype(i32))


def _final_norm_kernel(x_ref, w_ref, o_ref):
    x = x_ref[...]
    o_ref[...] = (x * lax.rsqrt(jnp.mean(x * x, axis=-1, keepdims=True) + EPS)) * w_ref[...]


def _final_norm(x, w):
    n, d = x.shape
    tm = min(TM_FFN, n)
    return pl.pallas_call(
        _final_norm_kernel,
        out_shape=jax.ShapeDtypeStruct((n, d), F32),
        grid=(n // tm,),
        in_specs=[pl.BlockSpec((tm, d), lambda i: (i, 0)), pl.BlockSpec((1, d), lambda i: (0, 0))],
        out_specs=pl.BlockSpec((tm, d), lambda i: (i, 0)),
        compiler_params=_params(1),
        name="final_norm",
    )(x, w)


def _retention_tables(ret_heads, seq):
    half = RET_HEAD_DIM // 2
    inv_freq = 1.0 / (ROPE_BASE ** (jnp.arange(half, dtype=F32) * 2.0 / RET_HEAD_DIM))
    ang = jnp.arange(seq, dtype=F32)[:, None] * inv_freq[None, :]
    log_gamma = jnp.log(1.0 - 2.0 ** (-5.0 - jnp.arange(ret_heads, dtype=F32)))
    j = jnp.arange(CHUNK, dtype=F32)
    diff = j[:, None] - j[None, :]
    dmask = jnp.where(diff >= 0, jnp.exp(log_gamma[:, None, None] * jnp.maximum(diff, 0.0)), 0.0)
    xi = jnp.exp(log_gamma[:, None] * (j[None, :] + 1.0))
    zeta = jnp.exp(log_gamma[:, None] * (CHUNK - 1.0 - j[None, :]))
    dec = jnp.exp(log_gamma * CHUNK)
    wide = (ret_heads, CHUNK, RET_HEAD_DIM)
    return (jnp.cos(ang), jnp.sin(ang), dmask,
            jnp.broadcast_to(xi[:, :, None], wide), jnp.broadcast_to(zeta[:, :, None], wide),
            jnp.broadcast_to(dec[:, None, None], (ret_heads, 1, RET_HEAD_DIM)))


def kernel(x, c, norm1_w, ada_w, ada_b, w_in, sg_w, sg_b, w_out, norm2_w, ffn_w_gate, ffn_w_up, ffn_w_down, router_w, router_b, moe_w_gate, moe_w_up, moe_w_down, final_norm_w):
    nb, seq, d = x.shape
    depth = w_in.shape[0]
    sg_heads = sg_w.shape[1]
    ret_heads = (w_in.shape[2] - 2 * sg_heads * SG_HEAD_DIM) // (4 * RET_HEAD_DIM)
    n = nb * seq
    assert seq % CHUNK == 0 and sg_w.shape[2] == CHUNK

    mod = _modulation(c, ada_w, ada_b).reshape(depth, nb, 6, d)
    cos, sin, dmask, xi, zeta, dec = _retention_tables(ret_heads, seq)
    xf = x.reshape(n, d)
    w_in_b, w_out_b = w_in.astype(BF16), w_out.astype(BF16)
    ffn_b = [w.astype(BF16) for w in (ffn_w_gate, ffn_w_up, ffn_w_down)]
    final_w = final_norm_w.reshape(1, d)
    for l in range(depth):
        z = _inproj(xf, norm1_w[l].reshape(1, d), mod[l], w_in_b, l, seq)
        sgb = jnp.broadcast_to(sg_b[l][:, :, None], (sg_heads, CHUNK, LANES))
        y = _mixer(z, cos, sin, sg_w[l], sgb, dmask, xi, zeta, dec, seq)
        i = l // 2
        nw2 = norm2_w[l].reshape(1, d)
        if l % 2 == 0:
            xf, h2 = _outproj(y, w_out_b, l, xf, mod[l], nw2, seq)
            xf = _ffn_dense(h2, *ffn_b, i, xf, mod[l], seq)
        else:
            n_experts = router_w.shape[2]
            tm = min(TM_FFN, n)
            xf, h2, route = _outproj(y, w_out_b, l, xf, mod[l], nw2, seq,
                                     router=(router_w[i], router_b[i]))
            tile_expert, n_used, src_token, p0, p1 = _routing_tables(route, n_experts, tm)
            hg = _gather_rows(h2, src_token, min(TG_GATHER, tm))
            o = _ffn_moe(tile_expert, n_used, hg, moe_w_gate, moe_w_up, moe_w_down, i, tm)
            xf = _combine(o, p0, p1, route, xf, mod[l], seq,
                          final_w=final_w if l == depth - 1 else None)
    if depth % 2 == 1:
        xf = _final_norm(xf, final_w)
    return xf.reshape(nb, seq, d)
```

```python
import functools

import jax
import jax.numpy as jnp
from jax import lax
from jax.experimental import pallas as pl
from jax.experimental.pallas import tpu as pltpu

F32 = jnp.float32
BF16 = jnp.bfloat16

EPS = 1e-6
ROPE_BASE = 10000.0
SG_HEAD_DIM = 128
RET_HEAD_DIM = 256
CHUNK = 128
TOP_K = 2
LANES = 128

TM_INPROJ = 1024
TN_INPROJ = 1024
TC_MIXER = 512
TM_OUTPROJ = 512
ROW_CHUNKS_OUTPROJ = 2
TM_FFN = 1024
TF_FFN = 512
TN_DOWN_DENSE = 512
TN_DOWN_MOE = 256
TG_GATHER = 1024
TR_COMBINE = 512
TN_MOD = 512
DMA_ISSUE_UNROLL = 8
VMEM_LIMIT = 60 * 1024 * 1024


def _params(n_axes):
    return pltpu.CompilerParams(dimension_semantics=("arbitrary",) * n_axes,
                                vmem_limit_bytes=VMEM_LIMIT)


def _gelu_tanh(x):
    return 0.5 * x * (1.0 + jnp.tanh(0.7978845608028654 * (x + 0.044715 * (x * x * x))))


def _silu(x):
    return x * jax.nn.sigmoid(x)


def _rms_modulate(x, nw, shift, scale):
    y = x * lax.rsqrt(jnp.mean(x * x, axis=-1, keepdims=True) + EPS)
    return (y * nw) * (1.0 + scale) + shift


def _mod_kernel(cb_ref, w_ref, b_ref, o_ref):
    tn = w_ref.shape[1]
    for b in range(cb_ref.shape[0]):
        act = _silu(cb_ref[b])
        cols = [jnp.sum(w_ref[:, j * LANES:(j + 1) * LANES] * act, axis=0, keepdims=True)
                for j in range(tn // LANES)]
        o_ref[b:b + 1, :] = jnp.concatenate(cols, axis=1) + b_ref[...]


def _modulation(c, ada_w, ada_b):
    depth, d, n6 = ada_w.shape
    nb = c.shape[0]
    tn = min(TN_MOD, n6)
    cb = jnp.broadcast_to(c[:, :, None], (nb, d, LANES))
    return pl.pallas_call(
        _mod_kernel,
        out_shape=jax.ShapeDtypeStruct((depth, nb, n6), F32),
        grid=(depth, n6 // tn),
        in_specs=[pl.BlockSpec((nb, d, LANES), lambda l, j: (0, 0, 0)),
                  pl.BlockSpec((None, d, tn), lambda l, j: (l, 0, j)),
                  pl.BlockSpec((None, 1, tn), lambda l, j: (l, 0, j))],
        out_specs=pl.BlockSpec((None, nb, tn), lambda l, j: (l, 0, j)),
        compiler_params=_params(2),
        name="adaln_mod",
    )(cb, ada_w, ada_b.reshape(depth, 1, n6))


def _inproj_kernel(x_ref, nw_ref, mod_ref, w_ref, z_ref, h_ref):
    @pl.when(pl.program_id(1) == 0)
    def _():
        h = _rms_modulate(x_ref[...], nw_ref[...], mod_ref[0:1, :], mod_ref[1:2, :])
        h_ref[...] = h.astype(BF16)

    z_ref[...] = jnp.dot(h_ref[...], w_ref[...], preferred_element_type=F32).astype(z_ref.dtype)


def _inproj(x, nw, mod, w, layer, seq, tm=TM_INPROJ, tn=TN_INPROJ):
    n, d = x.shape
    nin = w.shape[2]
    tm = min(tm, seq)
    tn = min(tn, nin)
    assert seq % tm == 0 and nin % tn == 0
    per_b = seq // tm
    w_mode = dict(pipeline_mode=pl.Buffered(1)) if tn == nin else {}
    return pl.pallas_call(
        _inproj_kernel,
        out_shape=jax.ShapeDtypeStruct((n, nin), BF16),
        grid=(n // tm, nin // tn),
        in_specs=[pl.BlockSpec((tm, d), lambda i, j: (i, 0)),
                  pl.BlockSpec((1, d), lambda i, j: (0, 0)),
                  pl.BlockSpec((None, 6, d), lambda i, j: (i // per_b, 0, 0)),
                  pl.BlockSpec((None, d, tn), lambda i, j: (layer, 0, j), **w_mode)],
        out_specs=pl.BlockSpec((tm, tn), lambda i, j: (i, j)),
        scratch_shapes=[pltpu.VMEM((tm, d), BF16)],
        compiler_params=_params(2),
        name="inproj",
    )(x, nw, mod, w)


def _mixer_kernel(z_ref, cos_ref, sin_ref, sgw_ref, sgb_ref, dmask_ref, xi_ref, zeta_ref,
                  dec_ref, y_ref, wm_ref, state_ref, *, sg_heads, ret_heads, steps_per_seq):
    sgw_w = sg_heads * SG_HEAD_DIM
    ret_w = ret_heads * RET_HEAD_DIM
    q_off, k_off = 2 * sgw_w, 2 * sgw_w + ret_w
    v_off, g_off = 2 * sgw_w + 2 * ret_w, 2 * sgw_w + 3 * ret_w
    half = RET_HEAD_DIM // 2
    step = pl.program_id(0)

    @pl.when(step == 0)
    def _():
        t = lax.broadcasted_iota(jnp.int32, (CHUNK, CHUNK), 0)
        s = lax.broadcasted_iota(jnp.int32, (CHUNK, CHUNK), 1)
        causal = (t >= s).astype(F32)
        for h in range(sg_heads):
            wm_ref[h] = (sgw_ref[h] * causal).astype(BF16)

    @pl.when(step % steps_per_seq == 0)
    def _():
        state_ref[...] = jnp.zeros_like(state_ref)

    def rotate(t, cos, sin):
        t1, t2 = t[:, :half], t[:, half:]
        return jnp.concatenate([t1 * cos - t2 * sin, t1 * sin + t2 * cos], axis=-1)

    def chunk(c, carry):
        rows = pl.ds(pl.multiple_of(c * CHUNK, CHUNK), CHUNK)
        cos = cos_ref[rows, :]
        sin = sin_ref[rows, :]
        for h in range(sg_heads):
            lo = h * SG_HEAD_DIM
            u = _gelu_tanh(z_ref[rows, lo:lo + SG_HEAD_DIM].astype(F32))
            v = _gelu_tanh(z_ref[rows, sgw_w + lo:sgw_w + lo + SG_HEAD_DIM].astype(F32))
            vc = v - jnp.mean(v, axis=-1, keepdims=True)
            vn = vc * lax.rsqrt(jnp.mean(vc * vc, axis=-1, keepdims=True) + EPS)
            sv = jnp.dot(wm_ref[h], vn.astype(BF16), preferred_element_type=F32) + sgb_ref[h]
            y_ref[rows, lo:lo + SG_HEAD_DIM] = (u * sv).astype(y_ref.dtype)
        for h in range(ret_heads):
            lo = h * RET_HEAD_DIM
            q = z_ref[rows, q_off + lo:q_off + lo + RET_HEAD_DIM].astype(F32)
            k = z_ref[rows, k_off + lo:k_off + lo + RET_HEAD_DIM].astype(F32)
            v = z_ref[rows, v_off + lo:v_off + lo + RET_HEAD_DIM]
            g = z_ref[rows, g_off + lo:g_off + lo + RET_HEAD_DIM].astype(F32)
            qr = rotate(q, cos, sin)
            kr = rotate(k, cos, sin) * (RET_HEAD_DIM ** -0.5)
            qb = qr.astype(BF16)
            scores = lax.dot_general(qb, kr.astype(BF16), (((1,), (1,)), ((), ())),
                                     preferred_element_type=F32) * dmask_ref[h]
            st = state_ref[h]
            yv = (jnp.dot(scores.astype(BF16), v, preferred_element_type=F32)
                  + jnp.dot(qb, st.astype(BF16), preferred_element_type=F32) * xi_ref[h])
            kz = (kr * zeta_ref[h]).astype(BF16)
            state_ref[h] = st * dec_ref[h] + lax.dot_general(
                kz, v, (((0,), (0,)), ((), ())), preferred_element_type=F32)
            yn = yv * lax.rsqrt(jnp.mean(yv * yv, axis=-1, keepdims=True) + EPS)
            y_ref[rows, sgw_w + lo:sgw_w + lo + RET_HEAD_DIM] = (_silu(g) * yn).astype(y_ref.dtype)
        return carry

    lax.fori_loop(0, z_ref.shape[0] // CHUNK, chunk, 0)


def _mixer(z, cos, sin, sgw, sgb, dmask, xi, zeta, dec, seq, tc=TC_MIXER):
    n, nin = z.shape
    sg_heads = sgw.shape[0]
    ret_heads = dmask.shape[0]
    mix = sg_heads * SG_HEAD_DIM + ret_heads * RET_HEAD_DIM
    tc = min(tc, seq)
    per_seq = seq // tc
    const3 = lambda i: (0, 0, 0)
    kern = functools.partial(_mixer_kernel, sg_heads=sg_heads, ret_heads=ret_heads,
                             steps_per_seq=per_seq)
    return pl.pallas_call(
        kern,
        out_shape=jax.ShapeDtypeStruct((n, mix), BF16),
        grid=(n // tc,),
        in_specs=[pl.BlockSpec((tc, nin), lambda i: (i, 0)),
                  pl.BlockSpec((tc, LANES), lambda i: (i % per_seq, 0)),
                  pl.BlockSpec((tc, LANES), lambda i: (i % per_seq, 0)),
                  pl.BlockSpec(sgw.shape, const3),
                  pl.BlockSpec(sgb.shape, const3),
                  pl.BlockSpec(dmask.shape, const3),
                  pl.BlockSpec(xi.shape, const3),
                  pl.BlockSpec(zeta.shape, const3),
                  pl.BlockSpec(dec.shape, const3)],
        out_specs=pl.BlockSpec((tc, mix), lambda i: (i, 0)),
        scratch_shapes=[pltpu.VMEM((sg_heads, CHUNK, CHUNK), BF16),
                        pltpu.VMEM((ret_heads, RET_HEAD_DIM, RET_HEAD_DIM), F32)],
        compiler_params=_params(1),
        name="mixer",
    )(z, cos, sin, sgw, sgb, dmask, xi, zeta, dec)


def _route_top2(logits):
    lane = lax.broadcasted_iota(jnp.int32, logits.shape, 1)
    m1 = jnp.max(logits, axis=-1, keepdims=True)
    i1 = jnp.min(jnp.where(logits == m1, lane, LANES), axis=-1, keepdims=True)
    rest = jnp.where(lane == i1, -jnp.inf, logits)
    m2 = jnp.max(rest, axis=-1, keepdims=True)
    i2 = jnp.min(jnp.where(rest == m2, lane, LANES), axis=-1, keepdims=True)
    e = jnp.exp(m2 - m1)
    den = 1.0 + e
    w1 = 1.0 / den
    w2 = e / den
    return jnp.where(lane == 0, i1.astype(F32),
                     jnp.where(lane == 1, i2.astype(F32),
                               jnp.where(lane == 2, w1, jnp.where(lane == 3, w2, 0.0))))


def _router_logits(h2, rwt_ref, rb_ref, n_experts):
    lane = lax.broadcasted_iota(jnp.int32, (h2.shape[0], LANES), 1)
    logits = jnp.full((h2.shape[0], LANES), -jnp.inf, F32)
    for e in range(n_experts):
        s = jnp.sum(h2 * rwt_ref[e:e + 1, :], axis=-1, keepdims=True)
        logits = jnp.where(lane == e, s, logits)
    return logits + rb_ref[...]


def _outproj_kernel(*refs, n_experts, row_chunks):
    if n_experts:
        y_ref, w_ref, x_ref, mod_ref, nw_ref, rwt_ref, rb_ref, xo_ref, h2_ref, r_ref = refs
    else:
        y_ref, w_ref, x_ref, mod_ref, nw_ref, xo_ref, h2_ref = refs
    rc = y_ref.shape[0] // row_chunks
    for c in range(row_chunks):
        rows = slice(c * rc, (c + 1) * rc)
        yo = jnp.dot(y_ref[rows, :], w_ref[...], preferred_element_type=F32)
        xn = x_ref[rows, :] + mod_ref[2:3, :] * yo
        xo_ref[rows, :] = xn
        h2 = _rms_modulate(xn, nw_ref[...], mod_ref[3:4, :], mod_ref[4:5, :])
        h2_ref[rows, :] = h2.astype(h2_ref.dtype)
        if n_experts:
            r_ref[rows, :] = _route_top2(_router_logits(h2, rwt_ref, rb_ref, n_experts))


def _outproj(y, w, layer, x, mod, nw, seq, router=None):
    n, d = x.shape
    mix = y.shape[1]
    tm = min(TM_OUTPROJ, seq)
    per_b = seq // tm
    row = lambda i: (i, 0)
    const = lambda i: (0, 0)
    in_specs = [pl.BlockSpec((tm, mix), row),
                pl.BlockSpec((None, mix, d), lambda i: (layer, 0, 0)),
                pl.BlockSpec((tm, d), row),
                pl.BlockSpec((None, 6, d), lambda i: (i // per_b, 0, 0)),
                pl.BlockSpec((1, d), const)]
    args = [y, w, x, mod, nw]
    out_shape = [jax.ShapeDtypeStruct((n, d), F32)]
    out_specs = [pl.BlockSpec((tm, d), row)]
    n_experts = 0
    if router is None:
        out_shape.append(jax.ShapeDtypeStruct((n, d), BF16))
        out_specs.append(pl.BlockSpec((tm, d), row))
    else:
        rw, rb = router
        n_experts = rw.shape[1]
        rb = jnp.pad(rb, (0, LANES - n_experts)).reshape(1, LANES)
        in_specs += [pl.BlockSpec((n_experts, d), const), pl.BlockSpec((1, LANES), const)]
        args += [rw.T, rb]
        out_shape += [jax.ShapeDtypeStruct((n, d), F32), jax.ShapeDtypeStruct((n, LANES), F32)]
        out_specs += [pl.BlockSpec((tm, d), row), pl.BlockSpec((tm, LANES), row)]
    return pl.pallas_call(
        functools.partial(_outproj_kernel, n_experts=n_experts, row_chunks=ROW_CHUNKS_OUTPROJ),
        out_shape=out_shape,
        grid=(n // tm,),
        in_specs=in_specs,
        out_specs=out_specs,
        compiler_params=_params(1),
        name="outproj_router" if n_experts else "outproj",
    )(*args)


def _merge_gate_up(wg, wu, tf):
    *lead, d, f = wg.shape
    split = (*lead, d, f // tf, tf)
    w = jnp.concatenate([wg.reshape(split), wu.reshape(split)], axis=-1)
    return jnp.moveaxis(w, -2, -3).astype(BF16)


def _tile_down(wd, tn):
    *lead, f, d = wd.shape
    return jnp.moveaxis(wd.reshape(*lead, f, d // tn, tn), -2, -3).astype(BF16)


def _ffn_up_step(h_ref, w_refs, a_ref, j, nj, tf):
    h = h_ref[...]
    if len(w_refs) == 1:
        gu = jnp.dot(h, w_refs[0][...].astype(BF16), preferred_element_type=F32)
        g, u = gu[:, :tf], gu[:, tf:]
    else:
        g = jnp.dot(h, w_refs[0][...].astype(BF16), preferred_element_type=F32)
        u = jnp.dot(h, w_refs[1][...].astype(BF16), preferred_element_type=F32)
    a = (_silu(g) * u).astype(BF16)
    for c in range(nj):
        @pl.when(j == c)
        def _():
            a_ref[:, c * tf:(c + 1) * tf] = a


def _ffn_down(a_ref, wd_ref):
    return jnp.dot(a_ref[...], wd_ref[...].astype(BF16), preferred_element_type=F32)


def _ffn_dense_kernel(*refs, nj, tf):
    h_ref, *w_refs, wd_ref, x_ref, mod_ref, o_ref, a_ref = refs
    j = pl.program_id(1)

    @pl.when(j < nj)
    def _():
        _ffn_up_step(h_ref, w_refs, a_ref, j, nj, tf)

    @pl.when(j >= nj)
    def _():
        o_ref[...] = x_ref[...] + mod_ref[5:6, :] * _ffn_down(a_ref, wd_ref)


def _ffn_dense(h2, weights, layer, x, mod, seq):
    n, d = x.shape
    tm = min(TM_FFN, seq)
    per_b = seq // tm
    if len(weights) == 2:
        nj, tf, nd, f, tn = (weights[0].shape[1], weights[0].shape[3] // 2, *weights[1].shape[1:])
        feat = lambda j: jnp.minimum(j, nj - 1)
        col = lambda j: jnp.maximum(j - nj, 0)
        w_specs = [pl.BlockSpec((None, None, d, 2 * tf), lambda i, j: (layer, feat(j), 0, 0)),
                   pl.BlockSpec((None, None, f, tn), lambda i, j: (layer, col(j), 0, 0))]
    else:
        f = weights[0].shape[2]
        tf, tn = min(TF_FFN, f), min(TN_DOWN_DENSE, d)
        nj, nd = f // tf, d // tn
        feat = lambda j: jnp.minimum(j, nj - 1)
        col = lambda j: jnp.maximum(j - nj, 0)
        w_specs = [pl.BlockSpec((None, d, tf), lambda i, j: (layer, 0, feat(j))),
                   pl.BlockSpec((None, d, tf), lambda i, j: (layer, 0, feat(j))),
                   pl.BlockSpec((None, f, tn), lambda i, j: (layer, 0, col(j)))]
    return pl.pallas_call(
        functools.partial(_ffn_dense_kernel, nj=nj, tf=tf),
        out_shape=jax.ShapeDtypeStruct((n, d), F32),
        grid=(n // tm, nj + nd),
        in_specs=[pl.BlockSpec((tm, d), lambda i, j: (i, 0)), *w_specs,
                  pl.BlockSpec((tm, tn), lambda i, j: (i, col(j))),
                  pl.BlockSpec((None, 6, tn), lambda i, j: (i // per_b, 0, col(j)))],
        out_specs=pl.BlockSpec((tm, tn), lambda i, j: (i, col(j))),
        scratch_shapes=[pltpu.VMEM((tm, f), BF16)],
        compiler_params=_params(2),
        name="ffn_dense",
    )(h2, *weights, x, mod)


def _ffn_moe_kernel(*refs, nj, tf):
    te_ref, nu_ref, h_ref, *w_refs, wd_ref, o_ref, a_ref = refs
    j = pl.program_id(1)
    active = pl.program_id(0) < nu_ref[0]

    @pl.when(jnp.logical_and(active, j < nj))
    def _():
        _ffn_up_step(h_ref, w_refs, a_ref, j, nj, tf)

    @pl.when(jnp.logical_and(active, j >= nj))
    def _():
        o_ref[...] = _ffn_down(a_ref, wd_ref)

    @pl.when(jnp.logical_and(jnp.logical_not(active), j >= nj))
    def _():
        o_ref[...] = jnp.zeros_like(o_ref)


def _ffn_moe(tile_expert, n_used, hg, weights, layer, tm):
    r, d = hg.shape
    merged = len(weights) == 2
    if merged:
        nj, tf, nd, f, tn = (weights[0].shape[2], weights[0].shape[4] // 2, *weights[1].shape[2:])
    else:
        f = weights[0].shape[3]
        tf, tn = min(TF_FFN, f), min(TN_DOWN_MOE, d)
        nj, nd = f // tf, d // tn

    def tile(i, nu):
        return jnp.minimum(i, nu[0] - 1)

    def feat(i, j, nu):
        return jnp.where(i < nu[0], jnp.minimum(j, nj - 1), nj - 1)

    def col_in(i, j, nu):
        return jnp.where(i < nu[0], jnp.maximum(j - nj, 0), nd - 1)

    if merged:
        w_specs = [pl.BlockSpec((None, None, None, d, 2 * tf),
                                lambda i, j, te, nu: (layer, te[i], feat(i, j, nu), 0, 0)),
                   pl.BlockSpec((None, None, None, f, tn),
                                lambda i, j, te, nu: (layer, te[i], col_in(i, j, nu), 0, 0))]
    else:
        up_spec = pl.BlockSpec((None, None, d, tf),
                               lambda i, j, te, nu: (layer, te[i], 0, feat(i, j, nu)))
        w_specs = [up_spec, up_spec,
                   pl.BlockSpec((None, None, f, tn),
                                lambda i, j, te, nu: (layer, te[i], 0, col_in(i, j, nu)))]
    grid_spec = pltpu.PrefetchScalarGridSpec(
        num_scalar_prefetch=2,
        grid=(r // tm, nj + nd),
        in_specs=[pl.BlockSpec((tm, d), lambda i, j, te, nu: (tile(i, nu), 0)), *w_specs],
        out_specs=pl.BlockSpec((tm, tn), lambda i, j, te, nu: (i, jnp.maximum(j - nj, 0))),
        scratch_shapes=[pltpu.VMEM((tm, f), BF16)],
    )
    return pl.pallas_call(
        functools.partial(_ffn_moe_kernel, nj=nj, tf=tf),
        out_shape=jax.ShapeDtypeStruct((r, d), F32),
        grid_spec=grid_spec,
        compiler_params=_params(2),
        name="ffn_moe",
    )(tile_expert, n_used, hg, *weights)


def _row_copy(src_hbm, dst_ref, src_row, dst_row, sem):
    return pltpu.make_async_copy(src_hbm.at[pl.ds(src_row, 1), :],
                                 dst_ref.at[pl.ds(dst_row, 1), :], sem)


def _gather_rows_kernel(idx_ref, src_hbm, o_ref, buf_ref, sem):
    rows = buf_ref.shape[0]

    def issue(k, carry):
        _row_copy(src_hbm, buf_ref, idx_ref[0, k], k, sem).start()
        return carry

    lax.fori_loop(0, rows, issue, 0, unroll=DMA_ISSUE_UNROLL)
    pltpu.make_async_copy(src_hbm.at[pl.ds(0, rows), :], buf_ref, sem).wait()
    o_ref[...] = buf_ref[...].astype(o_ref.dtype)


def _gather_rows(src, idx, tg):
    r = idx.shape[0]
    d = src.shape[1]
    return pl.pallas_call(
        _gather_rows_kernel,
        out_shape=jax.ShapeDtypeStruct((r, d), BF16),
        grid=(r // tg,),
        in_specs=[pl.BlockSpec((None, 1, tg), lambda i: (i, 0, 0), memory_space=pltpu.SMEM),
                  pl.BlockSpec(memory_space=pl.ANY)],
        out_specs=pl.BlockSpec((tg, d), lambda i: (i, 0)),
        scratch_shapes=[pltpu.VMEM((tg, d), src.dtype), pltpu.SemaphoreType.DMA],
        compiler_params=_params(1),
        name="gather_rows",
    )(idx.reshape(r // tg, 1, tg), src)


def _combine_kernel(*refs, final_norm):
    if final_norm:
        p0_ref, p1_ref, o_hbm, route_ref, x_ref, mod_ref, fw_ref, out_ref, buf0_ref, buf1_ref, sem = refs
    else:
        p0_ref, p1_ref, o_hbm, route_ref, x_ref, mod_ref, out_ref, buf0_ref, buf1_ref, sem = refs
    rows = buf0_ref.shape[0]

    def issue(k, carry):
        _row_copy(o_hbm, buf0_ref, p0_ref[0, k], k, sem).start()
        _row_copy(o_hbm, buf1_ref, p1_ref[0, k], k, sem).start()
        return carry

    lax.fori_loop(0, rows, issue, 0, unroll=DMA_ISSUE_UNROLL)
    pltpu.make_async_copy(o_hbm.at[pl.ds(0, rows), :], buf0_ref, sem).wait()
    pltpu.make_async_copy(o_hbm.at[pl.ds(0, rows), :], buf1_ref, sem).wait()
    w0 = route_ref[:, TOP_K:TOP_K + 1]
    w1 = route_ref[:, TOP_K + 1:TOP_K + 2]
    xn = x_ref[...] + mod_ref[5:6, :] * (w0 * buf0_ref[...] + w1 * buf1_ref[...])
    if final_norm:
        xn = (xn * lax.rsqrt(jnp.mean(xn * xn, axis=-1, keepdims=True) + EPS)) * fw_ref[...]
    out_ref[...] = xn


def _combine(o, p0, p1, route, x, mod, seq, final_w=None):
    n, d = x.shape
    tr = min(TR_COMBINE, seq)
    per_b = seq // tr
    idx_spec = pl.BlockSpec((None, 1, tr), lambda i: (i, 0, 0), memory_space=pltpu.SMEM)
    in_specs = [idx_spec, idx_spec,
                pl.BlockSpec(memory_space=pl.ANY),
                pl.BlockSpec((tr, LANES), lambda i: (i, 0)),
                pl.BlockSpec((tr, d), lambda i: (i, 0)),
                pl.BlockSpec((None, 6, d), lambda i: (i // per_b, 0, 0))]
    args = [p0.reshape(n // tr, 1, tr), p1.reshape(n // tr, 1, tr), o, route, x, mod]
    if final_w is not None:
        in_specs.append(pl.BlockSpec((1, d), lambda i: (0, 0)))
        args.append(final_w)
    return pl.pallas_call(
        functools.partial(_combine_kernel, final_norm=final_w is not None),
        out_shape=jax.ShapeDtypeStruct((n, d), F32),
        grid=(n // tr,),
        in_specs=in_specs,
        out_specs=pl.BlockSpec((tr, d), lambda i: (i, 0)),
        scratch_shapes=[pltpu.VMEM((tr, d), F32), pltpu.VMEM((tr, d), F32),
                        pltpu.SemaphoreType.DMA],
        compiler_params=_params(1),
        name="moe_combine",
    )(*args)


def _routing_tables(route, n_experts, tm):
    n = route.shape[0]
    i32 = jnp.int32
    ids = route[:, :TOP_K].astype(i32).reshape(-1)
    n_assign = n * TOP_K
    n_tiles = n_assign // tm + n_experts
    experts = jnp.arange(n_experts, dtype=i32)
    onehot = (ids[:, None] == experts[None, :]).astype(i32)
    ranks = jnp.cumsum(onehot, axis=0)
    counts = ranks[-1]
    offs = jnp.cumsum(counts) - counts
    tiles_per = (counts + tm - 1) // tm
    tile_end = jnp.cumsum(tiles_per)
    n_used = tile_end[-1]
    pad_offs = (tile_end - tiles_per) * tm
    pos = jnp.sum(onehot * (pad_offs[None, :] + ranks - 1), axis=1).reshape(n, TOP_K)
    tile_ids = jnp.arange(n_tiles, dtype=i32)
    tile_expert = jnp.sum((tile_ids[:, None] >= tile_end[None, :]).astype(i32), axis=1)
    last_expert = jnp.sum((n_used - 1 >= tile_end).astype(i32))
    tile_expert = jnp.where(tile_ids < n_used, tile_expert, last_expert).astype(i32)
    _, sorted_tok = lax.sort((ids, jnp.arange(n_assign, dtype=i32) // TOP_K), num_keys=1,
                             is_stable=True)
    row_expert = jnp.repeat(tile_expert, tm)
    rows = jnp.arange(n_tiles * tm, dtype=i32)
    shift = jnp.sum((row_expert[:, None] == experts[None, :]) * (offs - pad_offs)[None, :], axis=1)
    limit = jnp.sum((row_expert[:, None] == experts[None, :]) * (offs + counts)[None, :], axis=1)
    src = rows + shift
    valid = (src < limit) & (rows < n_used * tm)
    src_token = jnp.where(valid, sorted_tok[jnp.clip(src, 0, n_assign - 1)], 0).astype(i32)
    return (tile_expert, n_used.reshape(1).astype(i32), src_token,
            pos[:, 0].astype(i32), pos[:, 1].astype(i32))


def _final_norm_kernel(x_ref, w_ref, o_ref):
    x = x_ref[...]
    o_ref[...] = (x * lax.rsqrt(jnp.mean(x * x, axis=-1, keepdims=True) + EPS)) * w_ref[...]


def _final_norm(x, w):
    n, d = x.shape
    tm = min(TM_FFN, n)
    return pl.pallas_call(
        _final_norm_kernel,
        out_shape=jax.ShapeDtypeStruct((n, d), F32),
        grid=(n // tm,),
        in_specs=[pl.BlockSpec((tm, d), lambda i: (i, 0)), pl.BlockSpec((1, d), lambda i: (0, 0))],
        out_specs=pl.BlockSpec((tm, d), lambda i: (i, 0)),
        compiler_params=_params(1),
        name="final_norm",
    )(x, w)


def _retention_tables(ret_heads, seq):
    half = RET_HEAD_DIM // 2
    inv_freq = 1.0 / (ROPE_BASE ** (jnp.arange(half, dtype=F32) * 2.0 / RET_HEAD_DIM))
    ang = jnp.arange(seq, dtype=F32)[:, None] * inv_freq[None, :]
    log_gamma = jnp.log(1.0 - 2.0 ** (-5.0 - jnp.arange(ret_heads, dtype=F32)))
    j = jnp.arange(CHUNK, dtype=F32)
    diff = j[:, None] - j[None, :]
    dmask = jnp.where(diff >= 0, jnp.exp(log_gamma[:, None, None] * jnp.maximum(diff, 0.0)), 0.0)
    xi = jnp.exp(log_gamma[:, None] * (j[None, :] + 1.0))
    zeta = jnp.exp(log_gamma[:, None] * (CHUNK - 1.0 - j[None, :]))
    dec = jnp.exp(log_gamma * CHUNK)
    wide = (ret_heads, CHUNK, RET_HEAD_DIM)
    return (jnp.cos(ang), jnp.sin(ang), dmask,
            jnp.broadcast_to(xi[:, :, None], wide), jnp.broadcast_to(zeta[:, :, None], wide),
            jnp.broadcast_to(dec[:, None, None], (ret_heads, 1, RET_HEAD_DIM)))


def kernel(x, c, norm1_w, ada_w, ada_b, w_in, sg_w, sg_b, w_out, norm2_w, ffn_w_gate, ffn_w_up, ffn_w_down, router_w, router_b, moe_w_gate, moe_w_up, moe_w_down, final_norm_w):
    nb, seq, d = x.shape
    depth = w_in.shape[0]
    sg_heads = sg_w.shape[1]
    ret_heads = (w_in.shape[2] - 2 * sg_heads * SG_HEAD_DIM) // (4 * RET_HEAD_DIM)
    n = nb * seq
    assert seq % CHUNK == 0 and sg_w.shape[2] == CHUNK

    mod = _modulation(c, ada_w, ada_b).reshape(depth, nb, 6, d)
    cos, sin, dmask, xi, zeta, dec = _retention_tables(ret_heads, seq)
    xf = x.reshape(n, d)
    w_in_b, w_out_b = w_in.astype(BF16), w_out.astype(BF16)
    ffn_b = [w.astype(BF16) for w in (ffn_w_gate, ffn_w_up, ffn_w_down)]
    final_w = final_norm_w.reshape(1, d)
    inproj_tiles = [(1024, 1024), (1024, 2048), (512, 3072), (512, w_in.shape[2])]
    mixer_tiles = [512, 512, 1024, 1024]
    for l in range(depth):
        z = _inproj(xf, norm1_w[l].reshape(1, d), mod[l], w_in_b, l, seq, *inproj_tiles[l % 4])
        sgb = jnp.broadcast_to(sg_b[l][:, :, None], (sg_heads, CHUNK, LANES))
        y = _mixer(z, cos, sin, sg_w[l], sgb, dmask, xi, zeta, dec, seq, mixer_tiles[l % 4])
        i = l // 2
        nw2 = norm2_w[l].reshape(1, d)
        if l % 2 == 0:
            xf, h2 = _outproj(y, w_out_b, l, xf, mod[l], nw2, seq)
            if i % 2 == 0:
                xf = _ffn_dense(h2, ffn_b, i, xf, mod[l], seq)
            else:
                merged = (_merge_gate_up(ffn_w_gate[i:i + 1], ffn_w_up[i:i + 1], TF_FFN),
                          _tile_down(ffn_w_down[i:i + 1], TN_DOWN_DENSE))
                xf = _ffn_dense(h2, merged, 0, xf, mod[l], seq)
        else:
            n_experts = router_w.shape[2]
            tm = min(TM_FFN, n)
            xf, h2, route = _outproj(y, w_out_b, l, xf, mod[l], nw2, seq,
                                     router=(router_w[i], router_b[i]))
            tile_expert, n_used, src_token, p0, p1 = _routing_tables(route, n_experts, tm)
            hg = _gather_rows(h2, src_token, min(TG_GATHER, tm))
            if i % 2 == 0:
                o = _ffn_moe(tile_expert, n_used, hg, (moe_w_gate, moe_w_up, moe_w_down), i, tm)
            else:
                merged = (_merge_gate_up(moe_w_gate[i:i + 1], moe_w_up[i:i + 1], TF_FFN),
                          _tile_down(moe_w_down[i:i + 1], TN_DOWN_DENSE))
                o = _ffn_moe(tile_expert, n_used, hg, merged, 0, tm)
            xf = _combine(o, p0, p1, route, xf, mod[l], seq,
                          final_w=final_w if l == depth - 1 else None)
    if depth % 2 == 1:
        xf = _final_norm(xf, final_w)
    return xf.reshape(nb, seq, d)
```

```python
import functools

import jax
import jax.numpy as jnp
from jax import lax
from jax.experimental import pallas as pl
from jax.experimental.pallas import tpu as pltpu

F32 = jnp.float32
BF16 = jnp.bfloat16

EPS = 1e-6
ROPE_BASE = 10000.0
SG_HEAD_DIM = 128
RET_HEAD_DIM = 256
CHUNK = 128
TOP_K = 2
LANES = 128

TM_INPROJ = 512
TC_MIXER = 512
TM_OUTPROJ = 512
ROW_CHUNKS_OUTPROJ = 2
TM_FFN = 1024
TF_FFN = 512
TN_DOWN_DENSE = 512
TN_DOWN_MOE = 256
TG_GATHER = 1024
TR_COMBINE = 512
TN_MOD = 512
DMA_ISSUE_UNROLL = 8
VMEM_LIMIT = 60 * 1024 * 1024


def _params(n_axes):
    return pltpu.CompilerParams(dimension_semantics=("arbitrary",) * n_axes,
                                vmem_limit_bytes=VMEM_LIMIT)


def _gelu_tanh(x):
    return 0.5 * x * (1.0 + jnp.tanh(0.7978845608028654 * (x + 0.044715 * (x * x * x))))


def _silu(x):
    return x * jax.nn.sigmoid(x)


def _pack_bf16_pairs(x):
    k = x.shape[1] // 2
    lo = lax.bitcast_convert_type(x[:, :k].astype(BF16).astype(F32), jnp.uint32)
    hi = lax.bitcast_convert_type(x[:, k:].astype(BF16).astype(F32), jnp.uint32)
    return (lo >> 16) | (hi & jnp.uint32(0xFFFF0000))


def _unpack_bf16_pairs(p):
    lo = lax.bitcast_convert_type(p << 16, F32)
    hi = lax.bitcast_convert_type(p & jnp.uint32(0xFFFF0000), F32)
    return jnp.concatenate([lo.astype(BF16), hi.astype(BF16)], axis=1)


def _rms_modulate(x, nw, shift, scale):
    y = x * lax.rsqrt(jnp.mean(x * x, axis=-1, keepdims=True) + EPS)
    return (y * nw) * (1.0 + scale) + shift


def _mod_kernel(cb_ref, w_ref, b_ref, o_ref):
    tn = w_ref.shape[1]
    for b in range(cb_ref.shape[0]):
        act = _silu(cb_ref[b])
        cols = [jnp.sum(w_ref[:, j * LANES:(j + 1) * LANES] * act, axis=0, keepdims=True)
                for j in range(tn // LANES)]
        o_ref[b:b + 1, :] = jnp.concatenate(cols, axis=1) + b_ref[...]


def _modulation(c, ada_w, ada_b):
    depth, d, n6 = ada_w.shape
    nb = c.shape[0]
    tn = min(TN_MOD, n6)
    cb = jnp.broadcast_to(c[:, :, None], (nb, d, LANES))
    return pl.pallas_call(
        _mod_kernel,
        out_shape=jax.ShapeDtypeStruct((depth, nb, n6), F32),
        grid=(depth, n6 // tn),
        in_specs=[pl.BlockSpec((nb, d, LANES), lambda l, j: (0, 0, 0)),
                  pl.BlockSpec((None, d, tn), lambda l, j: (l, 0, j)),
                  pl.BlockSpec((None, 1, tn), lambda l, j: (l, 0, j))],
        out_specs=pl.BlockSpec((None, nb, tn), lambda l, j: (l, 0, j)),
        compiler_params=_params(2),
        name="adaln_mod",
    )(cb, ada_w, ada_b.reshape(depth, 1, n6))


def _inproj_kernel(x_ref, nw_ref, mod_ref, w_ref, z_ref):
    h = _rms_modulate(x_ref[...], nw_ref[...], mod_ref[0:1, :], mod_ref[1:2, :])
    z_ref[...] = jnp.dot(h.astype(BF16), w_ref[...], preferred_element_type=F32).astype(z_ref.dtype)


def _inproj(x, nw, mod, w, layer, seq):
    n, d = x.shape
    nin = w.shape[2]
    tm = min(TM_INPROJ, seq)
    assert seq % tm == 0
    per_b = seq // tm
    return pl.pallas_call(
        _inproj_kernel,
        out_shape=jax.ShapeDtypeStruct((n, nin), BF16),
        grid=(n // tm,),
        in_specs=[pl.BlockSpec((tm, d), lambda i: (i, 0)),
                  pl.BlockSpec((1, d), lambda i: (0, 0)),
                  pl.BlockSpec((None, 6, d), lambda i: (i // per_b, 0, 0)),
                  pl.BlockSpec((None, d, nin), lambda i: (layer, 0, 0), pipeline_mode=pl.Buffered(1))],
        out_specs=pl.BlockSpec((tm, nin), lambda i: (i, 0)),
        compiler_params=_params(1),
        name="inproj",
    )(x, nw, mod, w)


def _mixer_kernel(z_ref, cos_ref, sin_ref, sgw_ref, sgb_ref, dmask_ref, xi_ref, zeta_ref,
                  dec_ref, y_ref, wm_ref, state_ref, *, sg_heads, ret_heads, steps_per_seq):
    sgw_w = sg_heads * SG_HEAD_DIM
    ret_w = ret_heads * RET_HEAD_DIM
    q_off, k_off = 2 * sgw_w, 2 * sgw_w + ret_w
    v_off, g_off = 2 * sgw_w + 2 * ret_w, 2 * sgw_w + 3 * ret_w
    half = RET_HEAD_DIM // 2
    step = pl.program_id(0)

    @pl.when(step == 0)
    def _():
        t = lax.broadcasted_iota(jnp.int32, (CHUNK, CHUNK), 0)
        s = lax.broadcasted_iota(jnp.int32, (CHUNK, CHUNK), 1)
        causal = (t >= s).astype(F32)
        for h in range(sg_heads):
            wm_ref[h] = (sgw_ref[h] * causal).astype(BF16)

    @pl.when(step % steps_per_seq == 0)
    def _():
        state_ref[...] = jnp.zeros_like(state_ref)

    def rotate(t, cos, sin):
        t1, t2 = t[:, :half], t[:, half:]
        return jnp.concatenate([t1 * cos - t2 * sin, t1 * sin + t2 * cos], axis=-1)

    def chunk(c, carry):
        rows = pl.ds(pl.multiple_of(c * CHUNK, CHUNK), CHUNK)
        cos = cos_ref[rows, :]
        sin = sin_ref[rows, :]
        for h in range(sg_heads):
            lo = h * SG_HEAD_DIM
            u = _gelu_tanh(z_ref[rows, lo:lo + SG_HEAD_DIM].astype(F32))
            v = _gelu_tanh(z_ref[rows, sgw_w + lo:sgw_w + lo + SG_HEAD_DIM].astype(F32))
            vc = v - jnp.mean(v, axis=-1, keepdims=True)
            vn = vc * lax.rsqrt(jnp.mean(vc * vc, axis=-1, keepdims=True) + EPS)
            sv = jnp.dot(wm_ref[h], vn.astype(BF16), preferred_element_type=F32) + sgb_ref[h]
            y_ref[rows, lo:lo + SG_HEAD_DIM] = (u * sv).astype(y_ref.dtype)
        for h in range(ret_heads):
            lo = h * RET_HEAD_DIM
            q = z_ref[rows, q_off + lo:q_off + lo + RET_HEAD_DIM].astype(F32)
            k = z_ref[rows, k_off + lo:k_off + lo + RET_HEAD_DIM].astype(F32)
            v = z_ref[rows, v_off + lo:v_off + lo + RET_HEAD_DIM]
            g = z_ref[rows, g_off + lo:g_off + lo + RET_HEAD_DIM].astype(F32)
            qr = rotate(q, cos, sin)
            kr = rotate(k, cos, sin) * (RET_HEAD_DIM ** -0.5)
            qb = qr.astype(BF16)
            scores = lax.dot_general(qb, kr.astype(BF16), (((1,), (1,)), ((), ())),
                                     preferred_element_type=F32) * dmask_ref[h]
            st = state_ref[h]
            yv = (jnp.dot(scores.astype(BF16), v, preferred_element_type=F32)
                  + jnp.dot(qb, st.astype(BF16), preferred_element_type=F32) * xi_ref[h])
            kz = (kr * zeta_ref[h]).astype(BF16)
            state_ref[h] = st * dec_ref[h] + lax.dot_general(
                kz, v, (((0,), (0,)), ((), ())), preferred_element_type=F32)
            yn = yv * lax.rsqrt(jnp.mean(yv * yv, axis=-1, keepdims=True) + EPS)
            y_ref[rows, sgw_w + lo:sgw_w + lo + RET_HEAD_DIM] = (_silu(g) * yn).astype(y_ref.dtype)
        return carry

    lax.fori_loop(0, z_ref.shape[0] // CHUNK, chunk, 0)


def _mixer(z, cos, sin, sgw, sgb, dmask, xi, zeta, dec, seq):
    n, nin = z.shape
    sg_heads = sgw.shape[0]
    ret_heads = dmask.shape[0]
    mix = sg_heads * SG_HEAD_DIM + ret_heads * RET_HEAD_DIM
    tc = min(TC_MIXER, seq)
    per_seq = seq // tc
    const3 = lambda i: (0, 0, 0)
    kern = functools.partial(_mixer_kernel, sg_heads=sg_heads, ret_heads=ret_heads,
                             steps_per_seq=per_seq)
    return pl.pallas_call(
        kern,
        out_shape=jax.ShapeDtypeStruct((n, mix), BF16),
        grid=(n // tc,),
        in_specs=[pl.BlockSpec((tc, nin), lambda i: (i, 0)),
                  pl.BlockSpec((tc, LANES), lambda i: (i % per_seq, 0)),
                  pl.BlockSpec((tc, LANES), lambda i: (i % per_seq, 0)),
                  pl.BlockSpec(sgw.shape, const3),
                  pl.BlockSpec(sgb.shape, const3),
                  pl.BlockSpec(dmask.shape, const3),
                  pl.BlockSpec(xi.shape, const3),
                  pl.BlockSpec(zeta.shape, const3),
                  pl.BlockSpec(dec.shape, const3)],
        out_specs=pl.BlockSpec((tc, mix), lambda i: (i, 0)),
        scratch_shapes=[pltpu.VMEM((sg_heads, CHUNK, CHUNK), BF16),
                        pltpu.VMEM((ret_heads, RET_HEAD_DIM, RET_HEAD_DIM), F32)],
        compiler_params=_params(1),
        name="mixer",
    )(z, cos, sin, sgw, sgb, dmask, xi, zeta, dec)


def _route_top2(logits):
    lane = lax.broadcasted_iota(jnp.int32, logits.shape, 1)
    m1 = jnp.max(logits, axis=-1, keepdims=True)
    i1 = jnp.min(jnp.where(logits == m1, lane, LANES), axis=-1, keepdims=True)
    rest = jnp.where(lane == i1, -jnp.inf, logits)
    m2 = jnp.max(rest, axis=-1, keepdims=True)
    i2 = jnp.min(jnp.where(rest == m2, lane, LANES), axis=-1, keepdims=True)
    e = jnp.exp(m2 - m1)
    den = 1.0 + e
    w1 = 1.0 / den
    w2 = e / den
    return jnp.where(lane == 0, i1.astype(F32),
                     jnp.where(lane == 1, i2.astype(F32),
                               jnp.where(lane == 2, w1, jnp.where(lane == 3, w2, 0.0))))


def _router_logits(h2, rwt_ref, rb_ref, n_experts):
    lane = lax.broadcasted_iota(jnp.int32, (h2.shape[0], LANES), 1)
    logits = jnp.full((h2.shape[0], LANES), -jnp.inf, F32)
    for e in range(n_experts):
        s = jnp.sum(h2 * rwt_ref[e:e + 1, :], axis=-1, keepdims=True)
        logits = jnp.where(lane == e, s, logits)
    return logits + rb_ref[...]


def _outproj_kernel(*refs, n_experts, row_chunks):
    if n_experts:
        y_ref, w_ref, x_ref, mod_ref, nw_ref, rwt_ref, rb_ref, xo_ref, h2_ref, r_ref = refs
    else:
        y_ref, w_ref, x_ref, mod_ref, nw_ref, xo_ref, h2_ref = refs
    rc = y_ref.shape[0] // row_chunks
    for c in range(row_chunks):
        rows = slice(c * rc, (c + 1) * rc)
        yo = jnp.dot(y_ref[rows, :], w_ref[...], preferred_element_type=F32)
        xn = x_ref[rows, :] + mod_ref[2:3, :] * yo
        xo_ref[rows, :] = xn
        h2 = _rms_modulate(xn, nw_ref[...], mod_ref[3:4, :], mod_ref[4:5, :])
        if n_experts:
            h2_ref[rows, :] = _pack_bf16_pairs(h2)
            r_ref[rows, :] = _route_top2(_router_logits(h2, rwt_ref, rb_ref, n_experts))
        else:
            h2_ref[rows, :] = h2.astype(h2_ref.dtype)


def _outproj(y, w, layer, x, mod, nw, seq, router=None):
    n, d = x.shape
    mix = y.shape[1]
    tm = min(TM_OUTPROJ, seq)
    per_b = seq // tm
    row = lambda i: (i, 0)
    const = lambda i: (0, 0)
    in_specs = [pl.BlockSpec((tm, mix), row),
                pl.BlockSpec((None, mix, d), lambda i: (layer, 0, 0)),
                pl.BlockSpec((tm, d), row),
                pl.BlockSpec((None, 6, d), lambda i: (i // per_b, 0, 0)),
                pl.BlockSpec((1, d), const)]
    args = [y, w, x, mod, nw]
    out_shape = [jax.ShapeDtypeStruct((n, d), F32)]
    out_specs = [pl.BlockSpec((tm, d), row)]
    n_experts = 0
    if router is None:
        out_shape.append(jax.ShapeDtypeStruct((n, d), BF16))
        out_specs.append(pl.BlockSpec((tm, d), row))
    else:
        rw, rb = router
        n_experts = rw.shape[1]
        rb = jnp.pad(rb, (0, LANES - n_experts)).reshape(1, LANES)
        in_specs += [pl.BlockSpec((n_experts, d), const), pl.BlockSpec((1, LANES), const)]
        args += [rw.T, rb]
        out_shape += [jax.ShapeDtypeStruct((n, d // 2), jnp.uint32),
                      jax.ShapeDtypeStruct((n, LANES), F32)]
        out_specs += [pl.BlockSpec((tm, d // 2), row), pl.BlockSpec((tm, LANES), row)]
    return pl.pallas_call(
        functools.partial(_outproj_kernel, n_experts=n_experts, row_chunks=ROW_CHUNKS_OUTPROJ),
        out_shape=out_shape,
        grid=(n // tm,),
        in_specs=in_specs,
        out_specs=out_specs,
        compiler_params=_params(1),
        name="outproj_router" if n_experts else "outproj",
    )(*args)


def _ffn_up_step(h_ref, wg_ref, wu_ref, a_ref, j, nj, tf):
    h = h_ref[...]
    g = jnp.dot(h, wg_ref[...].astype(BF16), preferred_element_type=F32)
    u = jnp.dot(h, wu_ref[...].astype(BF16), preferred_element_type=F32)
    a = (_silu(g) * u).astype(BF16)
    for c in range(nj):
        @pl.when(j == c)
        def _():
            a_ref[:, c * tf:(c + 1) * tf] = a


def _ffn_down(a_ref, wd_ref):
    return jnp.dot(a_ref[...], wd_ref[...].astype(BF16), preferred_element_type=F32)


def _ffn_dense_kernel(h_ref, wg_ref, wu_ref, wd_ref, x_ref, mod_ref, o_ref, a_ref, *, nj, tf):
    j = pl.program_id(1)

    @pl.when(j < nj)
    def _():
        _ffn_up_step(h_ref, wg_ref, wu_ref, a_ref, j, nj, tf)

    @pl.when(j >= nj)
    def _():
        o_ref[...] = x_ref[...] + mod_ref[5:6, :] * _ffn_down(a_ref, wd_ref)


def _ffn_dense(h2, wg, wu, wd, layer, x, mod, seq):
    n, d = x.shape
    f = wg.shape[2]
    tm = min(TM_FFN, seq)
    tf = min(TF_FFN, f)
    tn = min(TN_DOWN_DENSE, d)
    nj, nd = f // tf, d // tn
    per_b = seq // tm
    feat = lambda j: jnp.minimum(j, nj - 1)
    col = lambda j: jnp.maximum(j - nj, 0)
    return pl.pallas_call(
        functools.partial(_ffn_dense_kernel, nj=nj, tf=tf),
        out_shape=jax.ShapeDtypeStruct((n, d), F32),
        grid=(n // tm, nj + nd),
        in_specs=[pl.BlockSpec((tm, d), lambda i, j: (i, 0)),
                  pl.BlockSpec((None, d, tf), lambda i, j: (layer, 0, feat(j))),
                  pl.BlockSpec((None, d, tf), lambda i, j: (layer, 0, feat(j))),
                  pl.BlockSpec((None, f, tn), lambda i, j: (layer, 0, col(j))),
                  pl.BlockSpec((tm, tn), lambda i, j: (i, col(j))),
                  pl.BlockSpec((None, 6, tn), lambda i, j: (i // per_b, 0, col(j)))],
        out_specs=pl.BlockSpec((tm, tn), lambda i, j: (i, col(j))),
        scratch_shapes=[pltpu.VMEM((tm, f), BF16)],
        compiler_params=_params(2),
        name="ffn_dense",
    )(h2, wg, wu, wd, x, mod)


def _ffn_moe_kernel(te_ref, nu_ref, hp_ref, wg_ref, wu_ref, wd_ref, o_ref, a_ref, h_ref, *, nj, tf):
    j = pl.program_id(1)
    active = pl.program_id(0) < nu_ref[0]

    @pl.when(jnp.logical_and(active, j == 0))
    def _():
        h_ref[...] = _unpack_bf16_pairs(hp_ref[...])

    @pl.when(jnp.logical_and(active, j < nj))
    def _():
        _ffn_up_step(h_ref, wg_ref, wu_ref, a_ref, j, nj, tf)

    @pl.when(jnp.logical_and(active, j >= nj))
    def _():
        o_ref[...] = _ffn_down(a_ref, wd_ref)

    @pl.when(jnp.logical_and(jnp.logical_not(active), j >= nj))
    def _():
        o_ref[...] = jnp.zeros_like(o_ref)


def _ffn_moe(tile_expert, n_used, hg, wg, wu, wd, layer, tm):
    r = hg.shape[0]
    d, f = wg.shape[2:]
    tf = min(TF_FFN, f)
    tn = min(TN_DOWN_MOE, d)
    nj, nd = f // tf, d // tn

    def tile(i, nu):
        return jnp.minimum(i, nu[0] - 1)

    def feat(i, j, nu):
        return jnp.where(i < nu[0], jnp.minimum(j, nj - 1), nj - 1)

    def col_in(i, j, nu):
        return jnp.where(i < nu[0], jnp.maximum(j - nj, 0), nd - 1)

    up_spec = pl.BlockSpec((None, None, d, tf),
                           lambda i, j, te, nu: (layer, te[i], 0, feat(i, j, nu)))
    grid_spec = pltpu.PrefetchScalarGridSpec(
        num_scalar_prefetch=2,
        grid=(r // tm, nj + nd),
        in_specs=[pl.BlockSpec((tm, d // 2), lambda i, j, te, nu: (tile(i, nu), 0)),
                  up_spec, up_spec,
                  pl.BlockSpec((None, None, f, tn),
                               lambda i, j, te, nu: (layer, te[i], 0, col_in(i, j, nu)))],
        out_specs=pl.BlockSpec((tm, tn), lambda i, j, te, nu: (i, jnp.maximum(j - nj, 0))),
        scratch_shapes=[pltpu.VMEM((tm, f), BF16), pltpu.VMEM((tm, d), BF16)],
    )
    return pl.pallas_call(
        functools.partial(_ffn_moe_kernel, nj=nj, tf=tf),
        out_shape=jax.ShapeDtypeStruct((r, d), F32),
        grid_spec=grid_spec,
        compiler_params=_params(2),
        name="ffn_moe",
    )(tile_expert, n_used, hg, wg, wu, wd)


def _row_copy(src_hbm, dst_ref, src_row, dst_row, sem):
    return pltpu.make_async_copy(src_hbm.at[pl.ds(src_row, 1), :],
                                 dst_ref.at[pl.ds(dst_row, 1), :], sem)


def _gather_rows_kernel(idx_ref, src_hbm, o_ref, sem):
    rows = o_ref.shape[0]

    def issue(k, carry):
        _row_copy(src_hbm, o_ref, idx_ref[0, k], k, sem).start()
        return carry

    lax.fori_loop(0, rows, issue, 0, unroll=DMA_ISSUE_UNROLL)
    pltpu.make_async_copy(src_hbm.at[pl.ds(0, rows), :], o_ref, sem).wait()


def _gather_rows(src, idx, tg):
    r = idx.shape[0]
    d = src.shape[1]
    return pl.pallas_call(
        _gather_rows_kernel,
        out_shape=jax.ShapeDtypeStruct((r, d), src.dtype),
        grid=(r // tg,),
        in_specs=[pl.BlockSpec((None, 1, tg), lambda i: (i, 0, 0), memory_space=pltpu.SMEM),
                  pl.BlockSpec(memory_space=pl.ANY)],
        out_specs=pl.BlockSpec((tg, d), lambda i: (i, 0)),
        scratch_shapes=[pltpu.SemaphoreType.DMA],
        compiler_params=_params(1),
        name="gather_rows",
    )(idx.reshape(r // tg, 1, tg), src)


def _combine_kernel(*refs, final_norm):
    if final_norm:
        p0_ref, p1_ref, o_hbm, route_ref, x_ref, mod_ref, fw_ref, out_ref, buf0_ref, buf1_ref, sem = refs
    else:
        p0_ref, p1_ref, o_hbm, route_ref, x_ref, mod_ref, out_ref, buf0_ref, buf1_ref, sem = refs
    rows = buf0_ref.shape[0]

    def issue(k, carry):
        _row_copy(o_hbm, buf0_ref, p0_ref[0, k], k, sem).start()
        _row_copy(o_hbm, buf1_ref, p1_ref[0, k], k, sem).start()
        return carry

    lax.fori_loop(0, rows, issue, 0, unroll=DMA_ISSUE_UNROLL)
    pltpu.make_async_copy(o_hbm.at[pl.ds(0, rows), :], buf0_ref, sem).wait()
    pltpu.make_async_copy(o_hbm.at[pl.ds(0, rows), :], buf1_ref, sem).wait()
    w0 = route_ref[:, TOP_K:TOP_K + 1]
    w1 = route_ref[:, TOP_K + 1:TOP_K + 2]
    xn = x_ref[...] + mod_ref[5:6, :] * (w0 * buf0_ref[...] + w1 * buf1_ref[...])
    if final_norm:
        xn = (xn * lax.rsqrt(jnp.mean(xn * xn, axis=-1, keepdims=True) + EPS)) * fw_ref[...]
    out_ref[...] = xn


def _combine(o, p0, p1, route, x, mod, seq, final_w=None):
    n, d = x.shape
    tr = min(TR_COMBINE, seq)
    per_b = seq // tr
    idx_spec = pl.BlockSpec((None, 1, tr), lambda i: (i, 0, 0), memory_space=pltpu.SMEM)
    in_specs = [idx_spec, idx_spec,
                pl.BlockSpec(memory_space=pl.ANY),
                pl.BlockSpec((tr, LANES), lambda i: (i, 0)),
                pl.BlockSpec((tr, d), lambda i: (i, 0)),
                pl.BlockSpec((None, 6, d), lambda i: (i // per_b, 0, 0))]
    args = [p0.reshape(n // tr, 1, tr), p1.reshape(n // tr, 1, tr), o, route, x, mod]
    if final_w is not None:
        in_specs.append(pl.BlockSpec((1, d), lambda i: (0, 0)))
        args.append(final_w)
    return pl.pallas_call(
        functools.partial(_combine_kernel, final_norm=final_w is not None),
        out_shape=jax.ShapeDtypeStruct((n, d), F32),
        grid=(n // tr,),
        in_specs=in_specs,
        out_specs=pl.BlockSpec((tr, d), lambda i: (i, 0)),
        scratch_shapes=[pltpu.VMEM((tr, d), F32), pltpu.VMEM((tr, d), F32),
                        pltpu.SemaphoreType.DMA],
        compiler_params=_params(1),
        name="moe_combine",
    )(*args)


def _routing_tables(route, n_experts, tm):
    n = route.shape[0]
    i32 = jnp.int32
    ids = route[:, :TOP_K].astype(i32).reshape(-1)
    n_assign = n * TOP_K
    n_tiles = n_assign // tm + n_experts
    experts = jnp.arange(n_experts, dtype=i32)
    onehot = (ids[:, None] == experts[None, :]).astype(i32)
    ranks = jnp.cumsum(onehot, axis=0)
    counts = ranks[-1]
    offs = jnp.cumsum(counts) - counts
    tiles_per = (counts + tm - 1) // tm
    tile_end = jnp.cumsum(tiles_per)
    n_used = tile_end[-1]
    pad_offs = (tile_end - tiles_per) * tm
    pos = jnp.sum(onehot * (pad_offs[None, :] + ranks - 1), axis=1).reshape(n, TOP_K)
    tile_ids = jnp.arange(n_tiles, dtype=i32)
    tile_expert = jnp.sum((tile_ids[:, None] >= tile_end[None, :]).astype(i32), axis=1)
    last_expert = jnp.sum((n_used - 1 >= tile_end).astype(i32))
    tile_expert = jnp.where(tile_ids < n_used, tile_expert, last_expert).astype(i32)
    _, sorted_tok = lax.sort((ids, jnp.arange(n_assign, dtype=i32) // TOP_K), num_keys=1,
                             is_stable=True)
    row_expert = jnp.repeat(tile_expert, tm)
    rows = jnp.arange(n_tiles * tm, dtype=i32)
    shift = jnp.sum((row_expert[:, None] == experts[None, :]) * (offs - pad_offs)[None, :], axis=1)
    limit = jnp.sum((row_expert[:, None] == experts[None, :]) * (offs + counts)[None, :], axis=1)
    src = rows + shift
    valid = (src < limit) & (rows < n_used * tm)
    src_token = jnp.where(valid, sorted_tok[jnp.clip(src, 0, n_assign - 1)], 0).astype(i32)
    return (tile_expert, n_used.reshape(1).astype(i32), src_token,
            pos[:, 0].astype(i32), pos[:, 1].astype(i32))


def _final_norm_kernel(x_ref, w_ref, o_ref):
    x = x_ref[...]
    o_ref[...] = (x * lax.rsqrt(jnp.mean(x * x, axis=-1, keepdims=True) + EPS)) * w_ref[...]


def _final_norm(x, w):
    n, d = x.shape
    tm = min(TM_FFN, n)
    return pl.pallas_call(
        _final_norm_kernel,
        out_shape=jax.ShapeDtypeStruct((n, d), F32),
        grid=(n // tm,),
        in_specs=[pl.BlockSpec((tm, d), lambda i: (i, 0)), pl.BlockSpec((1, d), lambda i: (0, 0))],
        out_specs=pl.BlockSpec((tm, d), lambda i: (i, 0)),
        compiler_params=_params(1),
        name="final_norm",
    )(x, w)


def _retention_tables(ret_heads, seq):
    half = RET_HEAD_DIM // 2
    inv_freq = 1.0 / (ROPE_BASE ** (jnp.arange(half, dtype=F32) * 2.0 / RET_HEAD_DIM))
    ang = jnp.arange(seq, dtype=F32)[:, None] * inv_freq[None, :]
    log_gamma = jnp.log(1.0 - 2.0 ** (-5.0 - jnp.arange(ret_heads, dtype=F32)))
    j = jnp.arange(CHUNK, dtype=F32)
    diff = j[:, None] - j[None, :]
    dmask = jnp.where(diff >= 0, jnp.exp(log_gamma[:, None, None] * jnp.maximum(diff, 0.0)), 0.0)
    xi = jnp.exp(log_gamma[:, None] * (j[None, :] + 1.0))
    zeta = jnp.exp(log_gamma[:, None] * (CHUNK - 1.0 - j[None, :]))
    dec = jnp.exp(log_gamma * CHUNK)
    wide = (ret_heads, CHUNK, RET_HEAD_DIM)
    return (jnp.cos(ang), jnp.sin(ang), dmask,
            jnp.broadcast_to(xi[:, :, None], wide), jnp.broadcast_to(zeta[:, :, None], wide),
            jnp.broadcast_to(dec[:, None, None], (ret_heads, 1, RET_HEAD_DIM)))


def kernel(x, c, norm1_w, ada_w, ada_b, w_in, sg_w, sg_b, w_out, norm2_w, ffn_w_gate, ffn_w_up, ffn_w_down, router_w, router_b, moe_w_gate, moe_w_up, moe_w_down, final_norm_w):
    nb, seq, d = x.shape
    depth = w_in.shape[0]
    sg_heads = sg_w.shape[1]
    ret_heads = (w_in.shape[2] - 2 * sg_heads * SG_HEAD_DIM) // (4 * RET_HEAD_DIM)
    n = nb * seq
    assert seq % CHUNK == 0 and sg_w.shape[2] == CHUNK

    mod = _modulation(c, ada_w, ada_b).reshape(depth, nb, 6, d)
    cos, sin, dmask, xi, zeta, dec = _retention_tables(ret_heads, seq)
    xf = x.reshape(n, d)
    w_in_b, w_out_b = w_in.astype(BF16), w_out.astype(BF16)
    ffn_b = [w.astype(BF16) for w in (ffn_w_gate, ffn_w_up, ffn_w_down)]
    final_w = final_norm_w.reshape(1, d)
    for l in range(depth):
        z = _inproj(xf, norm1_w[l].reshape(1, d), mod[l], w_in_b, l, seq)
        sgb = jnp.broadcast_to(sg_b[l][:, :, None], (sg_heads, CHUNK, LANES))
        y = _mixer(z, cos, sin, sg_w[l], sgb, dmask, xi, zeta, dec, seq)
        i = l // 2
        nw2 = norm2_w[l].reshape(1, d)
        if l % 2 == 0:
            xf, h2 = _outproj(y, w_out_b, l, xf, mod[l], nw2, seq)
            xf = _ffn_dense(h2, *ffn_b, i, xf, mod[l], seq)
        else:
            n_experts = router_w.shape[2]
            tm = min(TM_FFN, n)
            xf, h2, route = _outproj(y, w_out_b, l, xf, mod[l], nw2, seq,
                                     router=(router_w[i], router_b[i]))
            tile_expert, n_used, src_token, p0, p1 = _routing_tables(route, n_experts, tm)
            hg = _gather_rows(h2, src_token, min(TG_GATHER, tm))
            o = _ffn_moe(tile_expert, n_used, hg, moe_w_gate, moe_w_up, moe_w_down, i, tm)
            xf = _combine(o, p0, p1, route, xf, mod[l], seq,
                          final_w=final_w if l == depth - 1 else None)
    if depth % 2 == 1:
        xf = _final_norm(xf, final_w)
    return xf.reshape(nb, seq, d)
```

```python
import functools

import jax
import jax.numpy as jnp
from jax import lax
from jax.experimental import pallas as pl
from jax.experimental.pallas import tpu as pltpu

F32 = jnp.float32
BF16 = jnp.bfloat16

EPS = 1e-6
ROPE_BASE = 10000.0
SG_HEAD_DIM = 128
RET_HEAD_DIM = 256
CHUNK = 128
TOP_K = 2
LANES = 128

TM_INPROJ = 512
TC_MIXER = 512
TM_OUTPROJ = 512
ROW_CHUNKS_OUTPROJ = 2
TM_FFN = 1024
TF_FFN = 512
TN_DOWN_DENSE = 512
TN_DOWN_MOE = 256
ROW_BLOCK_MOE = 256
GATHER_ROWS_PER_STEP = 128
TR_COMBINE = 512
TN_MOD = 512
DMA_ISSUE_UNROLL = 8
VMEM_LIMIT = 60 * 1024 * 1024


def _params(n_axes):
    return pltpu.CompilerParams(dimension_semantics=("arbitrary",) * n_axes,
                                vmem_limit_bytes=VMEM_LIMIT)


def _gelu_tanh(x):
    return 0.5 * x * (1.0 + jnp.tanh(0.7978845608028654 * (x + 0.044715 * (x * x * x))))


def _silu(x):
    return x * jax.nn.sigmoid(x)


def _pack_bf16_pairs(x):
    k = x.shape[1] // 2
    lo = lax.bitcast_convert_type(x[:, :k].astype(BF16).astype(F32), jnp.uint32)
    hi = lax.bitcast_convert_type(x[:, k:].astype(BF16).astype(F32), jnp.uint32)
    return (lo >> 16) | (hi & jnp.uint32(0xFFFF0000))


def _unpack_bf16_pairs(p):
    lo = lax.bitcast_convert_type(p << 16, F32)
    hi = lax.bitcast_convert_type(p & jnp.uint32(0xFFFF0000), F32)
    return jnp.concatenate([lo.astype(BF16), hi.astype(BF16)], axis=1)


def _rms_modulate(x, nw, shift, scale):
    y = x * lax.rsqrt(jnp.mean(x * x, axis=-1, keepdims=True) + EPS)
    return (y * nw) * (1.0 + scale) + shift


def _mod_kernel(cb_ref, w_ref, b_ref, o_ref):
    tn = w_ref.shape[1]
    for b in range(cb_ref.shape[0]):
        act = _silu(cb_ref[b])
        cols = [jnp.sum(w_ref[:, j * LANES:(j + 1) * LANES] * act, axis=0, keepdims=True)
                for j in range(tn // LANES)]
        o_ref[b:b + 1, :] = jnp.concatenate(cols, axis=1) + b_ref[...]


def _modulation(c, ada_w, ada_b):
    depth, d, n6 = ada_w.shape
    nb = c.shape[0]
    tn = min(TN_MOD, n6)
    cb = jnp.broadcast_to(c[:, :, None], (nb, d, LANES))
    return pl.pallas_call(
        _mod_kernel,
        out_shape=jax.ShapeDtypeStruct((depth, nb, n6), F32),
        grid=(depth, n6 // tn),
        in_specs=[pl.BlockSpec((nb, d, LANES), lambda l, j: (0, 0, 0)),
                  pl.BlockSpec((None, d, tn), lambda l, j: (l, 0, j)),
                  pl.BlockSpec((None, 1, tn), lambda l, j: (l, 0, j))],
        out_specs=pl.BlockSpec((None, nb, tn), lambda l, j: (l, 0, j)),
        compiler_params=_params(2),
        name="adaln_mod",
    )(cb, ada_w, ada_b.reshape(depth, 1, n6))


def _inproj_kernel(x_ref, nw_ref, mod_ref, w_ref, z_ref):
    h = _rms_modulate(x_ref[...], nw_ref[...], mod_ref[0:1, :], mod_ref[1:2, :])
    z_ref[...] = jnp.dot(h.astype(BF16), w_ref[...], preferred_element_type=F32).astype(z_ref.dtype)


def _inproj(x, nw, mod, w, layer, seq):
    n, d = x.shape
    nin = w.shape[2]
    tm = min(TM_INPROJ, seq)
    assert seq % tm == 0
    per_b = seq // tm
    return pl.pallas_call(
        _inproj_kernel,
        out_shape=jax.ShapeDtypeStruct((n, nin), BF16),
        grid=(n // tm,),
        in_specs=[pl.BlockSpec((tm, d), lambda i: (i, 0)),
                  pl.BlockSpec((1, d), lambda i: (0, 0)),
                  pl.BlockSpec((None, 6, d), lambda i: (i // per_b, 0, 0)),
                  pl.BlockSpec((None, d, nin), lambda i: (layer, 0, 0), pipeline_mode=pl.Buffered(1))],
        out_specs=pl.BlockSpec((tm, nin), lambda i: (i, 0)),
        compiler_params=_params(1),
        name="inproj",
    )(x, nw, mod, w)


def _mixer_kernel(z_ref, cos_ref, sin_ref, sgw_ref, sgb_ref, dmask_ref, xi_ref, zeta_ref,
                  dec_ref, y_ref, wm_ref, state_ref, *, sg_heads, ret_heads, steps_per_seq):
    sgw_w = sg_heads * SG_HEAD_DIM
    ret_w = ret_heads * RET_HEAD_DIM
    q_off, k_off = 2 * sgw_w, 2 * sgw_w + ret_w
    v_off, g_off = 2 * sgw_w + 2 * ret_w, 2 * sgw_w + 3 * ret_w
    half = RET_HEAD_DIM // 2
    step = pl.program_id(0)

    @pl.when(step == 0)
    def _():
        t = lax.broadcasted_iota(jnp.int32, (CHUNK, CHUNK), 0)
        s = lax.broadcasted_iota(jnp.int32, (CHUNK, CHUNK), 1)
        causal = (t >= s).astype(F32)
        for h in range(sg_heads):
            wm_ref[h] = (sgw_ref[h] * causal).astype(BF16)

    @pl.when(step % steps_per_seq == 0)
    def _():
        state_ref[...] = jnp.zeros_like(state_ref)

    def rotate(t, cos, sin):
        t1, t2 = t[:, :half], t[:, half:]
        return jnp.concatenate([t1 * cos - t2 * sin, t1 * sin + t2 * cos], axis=-1)

    def chunk(c, carry):
        rows = pl.ds(pl.multiple_of(c * CHUNK, CHUNK), CHUNK)
        cos = cos_ref[rows, :]
        sin = sin_ref[rows, :]
        for h in range(sg_heads):
            lo = h * SG_HEAD_DIM
            u = _gelu_tanh(z_ref[rows, lo:lo + SG_HEAD_DIM].astype(F32))
            v = _gelu_tanh(z_ref[rows, sgw_w + lo:sgw_w + lo + SG_HEAD_DIM].astype(F32))
            vc = v - jnp.mean(v, axis=-1, keepdims=True)
            vn = vc * lax.rsqrt(jnp.mean(vc * vc, axis=-1, keepdims=True) + EPS)
            sv = jnp.dot(wm_ref[h], vn.astype(BF16), preferred_element_type=F32) + sgb_ref[h]
            y_ref[rows, lo:lo + SG_HEAD_DIM] = (u * sv).astype(y_ref.dtype)
        for h in range(ret_heads):
            lo = h * RET_HEAD_DIM
            q = z_ref[rows, q_off + lo:q_off + lo + RET_HEAD_DIM].astype(F32)
            k = z_ref[rows, k_off + lo:k_off + lo + RET_HEAD_DIM].astype(F32)
            v = z_ref[rows, v_off + lo:v_off + lo + RET_HEAD_DIM]
            g = z_ref[rows, g_off + lo:g_off + lo + RET_HEAD_DIM].astype(F32)
            qr = rotate(q, cos, sin)
            kr = rotate(k, cos, sin) * (RET_HEAD_DIM ** -0.5)
            qb = qr.astype(BF16)
            scores = lax.dot_general(qb, kr.astype(BF16), (((1,), (1,)), ((), ())),
                                     preferred_element_type=F32) * dmask_ref[h]
            st = state_ref[h]
            yv = (jnp.dot(scores.astype(BF16), v, preferred_element_type=F32)
                  + jnp.dot(qb, st.astype(BF16), preferred_element_type=F32) * xi_ref[h])
            kz = (kr * zeta_ref[h]).astype(BF16)
            state_ref[h] = st * dec_ref[h] + lax.dot_general(
                kz, v, (((0,), (0,)), ((), ())), preferred_element_type=F32)
            yn = yv * lax.rsqrt(jnp.mean(yv * yv, axis=-1, keepdims=True) + EPS)
            y_ref[rows, sgw_w + lo:sgw_w + lo + RET_HEAD_DIM] = (_silu(g) * yn).astype(y_ref.dtype)
        return carry

    lax.fori_loop(0, z_ref.shape[0] // CHUNK, chunk, 0)


def _mixer(z, cos, sin, sgw, sgb, dmask, xi, zeta, dec, seq):
    n, nin = z.shape
    sg_heads = sgw.shape[0]
    ret_heads = dmask.shape[0]
    mix = sg_heads * SG_HEAD_DIM + ret_heads * RET_HEAD_DIM
    tc = min(TC_MIXER, seq)
    per_seq = seq // tc
    const3 = lambda i: (0, 0, 0)
    kern = functools.partial(_mixer_kernel, sg_heads=sg_heads, ret_heads=ret_heads,
                             steps_per_seq=per_seq)
    return pl.pallas_call(
        kern,
        out_shape=jax.ShapeDtypeStruct((n, mix), BF16),
        grid=(n // tc,),
        in_specs=[pl.BlockSpec((tc, nin), lambda i: (i, 0)),
                  pl.BlockSpec((tc, LANES), lambda i: (i % per_seq, 0)),
                  pl.BlockSpec((tc, LANES), lambda i: (i % per_seq, 0)),
                  pl.BlockSpec(sgw.shape, const3),
                  pl.BlockSpec(sgb.shape, const3),
                  pl.BlockSpec(dmask.shape, const3),
                  pl.BlockSpec(xi.shape, const3),
                  pl.BlockSpec(zeta.shape, const3),
                  pl.BlockSpec(dec.shape, const3)],
        out_specs=pl.BlockSpec((tc, mix), lambda i: (i, 0)),
        scratch_shapes=[pltpu.VMEM((sg_heads, CHUNK, CHUNK), BF16),
                        pltpu.VMEM((ret_heads, RET_HEAD_DIM, RET_HEAD_DIM), F32)],
        compiler_params=_params(1),
        name="mixer",
    )(z, cos, sin, sgw, sgb, dmask, xi, zeta, dec)


def _route_top2(logits):
    lane = lax.broadcasted_iota(jnp.int32, logits.shape, 1)
    m1 = jnp.max(logits, axis=-1, keepdims=True)
    i1 = jnp.min(jnp.where(logits == m1, lane, LANES), axis=-1, keepdims=True)
    rest = jnp.where(lane == i1, -jnp.inf, logits)
    m2 = jnp.max(rest, axis=-1, keepdims=True)
    i2 = jnp.min(jnp.where(rest == m2, lane, LANES), axis=-1, keepdims=True)
    e = jnp.exp(m2 - m1)
    den = 1.0 + e
    w1 = 1.0 / den
    w2 = e / den
    return jnp.where(lane == 0, i1.astype(F32),
                     jnp.where(lane == 1, i2.astype(F32),
                               jnp.where(lane == 2, w1, jnp.where(lane == 3, w2, 0.0))))


def _router_logits(h2, rwt_ref, rb_ref, n_experts):
    lane = lax.broadcasted_iota(jnp.int32, (h2.shape[0], LANES), 1)
    logits = jnp.full((h2.shape[0], LANES), -jnp.inf, F32)
    for e in range(n_experts):
        s = jnp.sum(h2 * rwt_ref[e:e + 1, :], axis=-1, keepdims=True)
        logits = jnp.where(lane == e, s, logits)
    return logits + rb_ref[...]


def _outproj_kernel(*refs, n_experts, row_chunks):
    if n_experts:
        y_ref, w_ref, x_ref, mod_ref, nw_ref, rwt_ref, rb_ref, xo_ref, h2_ref, r_ref = refs
    else:
        y_ref, w_ref, x_ref, mod_ref, nw_ref, xo_ref, h2_ref = refs
    rc = y_ref.shape[0] // row_chunks
    for c in range(row_chunks):
        rows = slice(c * rc, (c + 1) * rc)
        yo = jnp.dot(y_ref[rows, :], w_ref[...], preferred_element_type=F32)
        xn = x_ref[rows, :] + mod_ref[2:3, :] * yo
        xo_ref[rows, :] = xn
        h2 = _rms_modulate(xn, nw_ref[...], mod_ref[3:4, :], mod_ref[4:5, :])
        if n_experts:
            h2_ref[rows, :] = _pack_bf16_pairs(h2)
            r_ref[rows, :] = _route_top2(_router_logits(h2, rwt_ref, rb_ref, n_experts))
        else:
            h2_ref[rows, :] = h2.astype(h2_ref.dtype)


def _outproj(y, w, layer, x, mod, nw, seq, router=None):
    n, d = x.shape
    mix = y.shape[1]
    tm = min(TM_OUTPROJ, seq)
    per_b = seq // tm
    row = lambda i: (i, 0)
    const = lambda i: (0, 0)
    in_specs = [pl.BlockSpec((tm, mix), row),
                pl.BlockSpec((None, mix, d), lambda i: (layer, 0, 0)),
                pl.BlockSpec((tm, d), row),
                pl.BlockSpec((None, 6, d), lambda i: (i // per_b, 0, 0)),
                pl.BlockSpec((1, d), const)]
    args = [y, w, x, mod, nw]
    out_shape = [jax.ShapeDtypeStruct((n, d), F32)]
    out_specs = [pl.BlockSpec((tm, d), row)]
    n_experts = 0
    if router is None:
        out_shape.append(jax.ShapeDtypeStruct((n, d), BF16))
        out_specs.append(pl.BlockSpec((tm, d), row))
    else:
        rw, rb = router
        n_experts = rw.shape[1]
        rb = jnp.pad(rb, (0, LANES - n_experts)).reshape(1, LANES)
        in_specs += [pl.BlockSpec((n_experts, d), const), pl.BlockSpec((1, LANES), const)]
        args += [rw.T, rb]
        out_shape += [jax.ShapeDtypeStruct((n, d // 2), jnp.uint32),
                      jax.ShapeDtypeStruct((n, LANES), F32)]
        out_specs += [pl.BlockSpec((tm, d // 2), row), pl.BlockSpec((tm, LANES), row)]
    return pl.pallas_call(
        functools.partial(_outproj_kernel, n_experts=n_experts, row_chunks=ROW_CHUNKS_OUTPROJ),
        out_shape=out_shape,
        grid=(n // tm,),
        in_specs=in_specs,
        out_specs=out_specs,
        compiler_params=_params(1),
        name="outproj_router" if n_experts else "outproj",
    )(*args)


def _ffn_up_step(h_ref, wg_ref, wu_ref, a_ref, j, nj, tf, rows=None):
    rows = h_ref.shape[0] if rows is None else rows
    h = h_ref[:rows, :]
    g = jnp.dot(h, wg_ref[...].astype(BF16), preferred_element_type=F32)
    u = jnp.dot(h, wu_ref[...].astype(BF16), preferred_element_type=F32)
    a = (_silu(g) * u).astype(BF16)
    for c in range(nj):
        @pl.when(j == c)
        def _():
            a_ref[:rows, c * tf:(c + 1) * tf] = a


def _ffn_down(a_ref, wd_ref, rows=None):
    rows = a_ref.shape[0] if rows is None else rows
    return jnp.dot(a_ref[:rows, :], wd_ref[...].astype(BF16), preferred_element_type=F32)


def _ffn_dense_kernel(h_ref, wg_ref, wu_ref, wd_ref, x_ref, mod_ref, o_ref, a_ref, *, nj, tf):
    j = pl.program_id(1)

    @pl.when(j < nj)
    def _():
        _ffn_up_step(h_ref, wg_ref, wu_ref, a_ref, j, nj, tf)

    @pl.when(j >= nj)
    def _():
        o_ref[...] = x_ref[...] + mod_ref[5:6, :] * _ffn_down(a_ref, wd_ref)


def _ffn_dense(h2, wg, wu, wd, layer, x, mod, seq):
    n, d = x.shape
    f = wg.shape[2]
    tm = min(TM_FFN, seq)
    tf = min(TF_FFN, f)
    tn = min(TN_DOWN_DENSE, d)
    nj, nd = f // tf, d // tn
    per_b = seq // tm
    feat = lambda j: jnp.minimum(j, nj - 1)
    col = lambda j: jnp.maximum(j - nj, 0)
    return pl.pallas_call(
        functools.partial(_ffn_dense_kernel, nj=nj, tf=tf),
        out_shape=jax.ShapeDtypeStruct((n, d), F32),
        grid=(n // tm, nj + nd),
        in_specs=[pl.BlockSpec((tm, d), lambda i, j: (i, 0)),
                  pl.BlockSpec((None, d, tf), lambda i, j: (layer, 0, feat(j))),
                  pl.BlockSpec((None, d, tf), lambda i, j: (layer, 0, feat(j))),
                  pl.BlockSpec((None, f, tn), lambda i, j: (layer, 0, col(j))),
                  pl.BlockSpec((tm, tn), lambda i, j: (i, col(j))),
                  pl.BlockSpec((None, 6, tn), lambda i, j: (i // per_b, 0, col(j)))],
        out_specs=pl.BlockSpec((tm, tn), lambda i, j: (i, col(j))),
        scratch_shapes=[pltpu.VMEM((tm, f), BF16)],
        compiler_params=_params(2),
        name="ffn_dense",
    )(h2, wg, wu, wd, x, mod)


def _row_copy(src_hbm, dst_ref, src_row, dst_row, sem):
    return pltpu.make_async_copy(src_hbm.at[pl.ds(src_row, 1), :],
                                 dst_ref.at[pl.ds(dst_row, 1), :], sem)


def _ffn_moe_kernel(te_ref, nu_ref, nv_ref, src_ref, hp_hbm, wg_ref, wu_ref, wd_ref, o_ref,
                    a_ref, h_ref, stage_ref, sem, *, nj, tf):
    i = pl.program_id(0)
    j = pl.program_id(1)
    tm = h_ref.shape[0]
    n_used = nu_ref[0]
    active = i < n_used
    blocks = (nv_ref[i] + ROW_BLOCK_MOE - 1) // ROW_BLOCK_MOE

    def start_rows(tile, first, count):
        def body(k, carry):
            r = first + k
            _row_copy(hp_hbm, stage_ref, src_ref[tile * tm + r], r, sem).start()
            return carry
        lax.fori_loop(0, count, body, 0, unroll=DMA_ISSUE_UNROLL)

    @pl.when(jnp.logical_and(i == 0, j == 0))
    def _():
        start_rows(0, 0, tm)

    @pl.when(jnp.logical_and(active, j == 0))
    def _():
        pltpu.make_async_copy(hp_hbm.at[pl.ds(0, tm), :], stage_ref, sem).wait()
        h_ref[...] = _unpack_bf16_pairs(stage_ref[...])

    prefetching = jnp.logical_and(j >= 1, j <= tm // GATHER_ROWS_PER_STEP)

    @pl.when(jnp.logical_and(i + 1 < n_used, prefetching))
    def _():
        start_rows(i + 1, (j - 1) * GATHER_ROWS_PER_STEP, GATHER_ROWS_PER_STEP)

    for q in range(1, tm // ROW_BLOCK_MOE + 1):
        rows = q * ROW_BLOCK_MOE
        this = jnp.logical_and(active, blocks == q)

        @pl.when(jnp.logical_and(this, j < nj))
        def _():
            _ffn_up_step(h_ref, wg_ref, wu_ref, a_ref, j, nj, tf, rows=rows)

        @pl.when(jnp.logical_and(this, j >= nj))
        def _():
            o_ref[:rows, :] = _ffn_down(a_ref, wd_ref, rows=rows)
            if rows < tm:
                o_ref[rows:, :] = jnp.zeros((tm - rows, o_ref.shape[1]), o_ref.dtype)

    @pl.when(jnp.logical_and(jnp.logical_not(active), j >= nj))
    def _():
        o_ref[...] = jnp.zeros_like(o_ref)


def _ffn_moe(tile_expert, n_used, tile_rows, src_token, hp, wg, wu, wd, layer, tm):
    r = src_token.shape[0]
    d, f = wg.shape[2:]
    tf = min(TF_FFN, f)
    tn = min(TN_DOWN_MOE, d)
    nj, nd = f // tf, d // tn
    assert tm % ROW_BLOCK_MOE == 0 and tm % GATHER_ROWS_PER_STEP == 0
    assert tm // GATHER_ROWS_PER_STEP <= nj + nd - 1

    def feat(i, j, nu):
        return jnp.where(i < nu[0], jnp.minimum(j, nj - 1), nj - 1)

    def col_in(i, j, nu):
        return jnp.where(i < nu[0], jnp.maximum(j - nj, 0), nd - 1)

    up_spec = pl.BlockSpec((None, None, d, tf),
                           lambda i, j, te, nu, nv, src: (layer, te[i], 0, feat(i, j, nu)))
    grid_spec = pltpu.PrefetchScalarGridSpec(
        num_scalar_prefetch=4,
        grid=(r // tm, nj + nd),
        in_specs=[pl.BlockSpec(memory_space=pl.ANY),
                  up_spec, up_spec,
                  pl.BlockSpec((None, None, f, tn),
                               lambda i, j, te, nu, nv, src: (layer, te[i], 0, col_in(i, j, nu)))],
        out_specs=pl.BlockSpec((tm, tn), lambda i, j, te, nu, nv, src: (i, jnp.maximum(j - nj, 0))),
        scratch_shapes=[pltpu.VMEM((tm, f), BF16), pltpu.VMEM((tm, d), BF16),
                        pltpu.VMEM((tm, d // 2), jnp.uint32), pltpu.SemaphoreType.DMA],
    )
    return pl.pallas_call(
        functools.partial(_ffn_moe_kernel, nj=nj, tf=tf),
        out_shape=jax.ShapeDtypeStruct((r, d), F32),
        grid_spec=grid_spec,
        compiler_params=_params(2),
        name="ffn_moe",
    )(tile_expert, n_used, tile_rows, src_token, hp, wg, wu, wd)


def _combine_kernel(*refs, final_norm):
    if final_norm:
        p0_ref, p1_ref, o_hbm, route_ref, x_ref, mod_ref, fw_ref, out_ref, buf0_ref, buf1_ref, sem = refs
    else:
        p0_ref, p1_ref, o_hbm, route_ref, x_ref, mod_ref, out_ref, buf0_ref, buf1_ref, sem = refs
    rows = buf0_ref.shape[0]

    def issue(k, carry):
        _row_copy(o_hbm, buf0_ref, p0_ref[0, k], k, sem).start()
        _row_copy(o_hbm, buf1_ref, p1_ref[0, k], k, sem).start()
        return carry

    lax.fori_loop(0, rows, issue, 0, unroll=DMA_ISSUE_UNROLL)
    pltpu.make_async_copy(o_hbm.at[pl.ds(0, rows), :], buf0_ref, sem).wait()
    pltpu.make_async_copy(o_hbm.at[pl.ds(0, rows), :], buf1_ref, sem).wait()
    w0 = route_ref[:, TOP_K:TOP_K + 1]
    w1 = route_ref[:, TOP_K + 1:TOP_K + 2]
    xn = x_ref[...] + mod_ref[5:6, :] * (w0 * buf0_ref[...] + w1 * buf1_ref[...])
    if final_norm:
        xn = (xn * lax.rsqrt(jnp.mean(xn * xn, axis=-1, keepdims=True) + EPS)) * fw_ref[...]
    out_ref[...] = xn


def _combine(o, p0, p1, route, x, mod, seq, final_w=None):
    n, d = x.shape
    tr = min(TR_COMBINE, seq)
    per_b = seq // tr
    idx_spec = pl.BlockSpec((None, 1, tr), lambda i: (i, 0, 0), memory_space=pltpu.SMEM)
    in_specs = [idx_spec, idx_spec,
                pl.BlockSpec(memory_space=pl.ANY),
                pl.BlockSpec((tr, LANES), lambda i: (i, 0)),
                pl.BlockSpec((tr, d), lambda i: (i, 0)),
                pl.BlockSpec((None, 6, d), lambda i: (i // per_b, 0, 0))]
    args = [p0.reshape(n // tr, 1, tr), p1.reshape(n // tr, 1, tr), o, route, x, mod]
    if final_w is not None:
        in_specs.append(pl.BlockSpec((1, d), lambda i: (0, 0)))
        args.append(final_w)
    return pl.pallas_call(
        functools.partial(_combine_kernel, final_norm=final_w is not None),
        out_shape=jax.ShapeDtypeStruct((n, d), F32),
        grid=(n // tr,),
        in_specs=in_specs,
        out_specs=pl.BlockSpec((tr, d), lambda i: (i, 0)),
        scratch_shapes=[pltpu.VMEM((tr, d), F32), pltpu.VMEM((tr, d), F32),
                        pltpu.SemaphoreType.DMA],
        compiler_params=_params(1),
        name="moe_combine",
    )(*args)


def _routing_tables(route, n_experts, tm):
    n = route.shape[0]
    i32 = jnp.int32
    ids = route[:, :TOP_K].astype(i32).reshape(-1)
    n_assign = n * TOP_K
    n_tiles = n_assign // tm + n_experts
    experts = jnp.arange(n_experts, dtype=i32)
    onehot = (ids[:, None] == experts[None, :]).astype(i32)
    ranks = jnp.cumsum(onehot, axis=0)
    counts = ranks[-1]
    offs = jnp.cumsum(counts) - counts
    tiles_per = (counts + tm - 1) // tm
    tile_end = jnp.cumsum(tiles_per)
    n_used = tile_end[-1]
    pad_offs = (tile_end - tiles_per) * tm
    pos = jnp.sum(onehot * (pad_offs[None, :] + ranks - 1), axis=1).reshape(n, TOP_K)
    tile_ids = jnp.arange(n_tiles, dtype=i32)
    tile_expert = jnp.sum((tile_ids[:, None] >= tile_end[None, :]).astype(i32), axis=1)
    last_expert = jnp.sum((n_used - 1 >= tile_end).astype(i32))
    tile_expert = jnp.where(tile_ids < n_used, tile_expert, last_expert).astype(i32)
    _, sorted_tok = lax.sort((ids, jnp.arange(n_assign, dtype=i32) // TOP_K), num_keys=1,
                             is_stable=True)
    row_expert = jnp.repeat(tile_expert, tm)
    rows = jnp.arange(n_tiles * tm, dtype=i32)
    shift = jnp.sum((row_expert[:, None] == experts[None, :]) * (offs - pad_offs)[None, :], axis=1)
    limit = jnp.sum((row_expert[:, None] == experts[None, :]) * (offs + counts)[None, :], axis=1)
    src = rows + shift
    valid = (src < limit) & (rows < n_used * tm)
    src_token = jnp.where(valid, sorted_tok[jnp.clip(src, 0, n_assign - 1)], 0).astype(i32)
    tile_rows = jnp.sum(valid.reshape(n_tiles, tm).astype(i32), axis=1)
    return (tile_expert, n_used.reshape(1).astype(i32), tile_rows, src_token,
            pos[:, 0].astype(i32), pos[:, 1].astype(i32))


def _final_norm_kernel(x_ref, w_ref, o_ref):
    x = x_ref[...]
    o_ref[...] = (x * lax.rsqrt(jnp.mean(x * x, axis=-1, keepdims=True) + EPS)) * w_ref[...]


def _final_norm(x, w):
    n, d = x.shape
    tm = min(TM_FFN, n)
    return pl.pallas_call(
        _final_norm_kernel,
        out_shape=jax.ShapeDtypeStruct((n, d), F32),
        grid=(n // tm,),
        in_specs=[pl.BlockSpec((tm, d), lambda i: (i, 0)), pl.BlockSpec((1, d), lambda i: (0, 0))],
        out_specs=pl.BlockSpec((tm, d), lambda i: (i, 0)),
        compiler_params=_params(1),
        name="final_norm",
    )(x, w)


def _retention_tables(ret_heads, seq):
    half = RET_HEAD_DIM // 2
    inv_freq = 1.0 / (ROPE_BASE ** (jnp.arange(half, dtype=F32) * 2.0 / RET_HEAD_DIM))
    ang = jnp.arange(seq, dtype=F32)[:, None] * inv_freq[None, :]
    log_gamma = jnp.log(1.0 - 2.0 ** (-5.0 - jnp.arange(ret_heads, dtype=F32)))
    j = jnp.arange(CHUNK, dtype=F32)
    diff = j[:, None] - j[None, :]
    dmask = jnp.where(diff >= 0, jnp.exp(log_gamma[:, None, None] * jnp.maximum(diff, 0.0)), 0.0)
    xi = jnp.exp(log_gamma[:, None] * (j[None, :] + 1.0))
    zeta = jnp.exp(log_gamma[:, None] * (CHUNK - 1.0 - j[None, :]))
    dec = jnp.exp(log_gamma * CHUNK)
    wide = (ret_heads, CHUNK, RET_HEAD_DIM)
    return (jnp.cos(ang), jnp.sin(ang), dmask,
            jnp.broadcast_to(xi[:, :, None], wide), jnp.broadcast_to(zeta[:, :, None], wide),
            jnp.broadcast_to(dec[:, None, None], (ret_heads, 1, RET_HEAD_DIM)))


def kernel(x, c, norm1_w, ada_w, ada_b, w_in, sg_w, sg_b, w_out, norm2_w, ffn_w_gate, ffn_w_up, ffn_w_down, router_w, router_b, moe_w_gate, moe_w_up, moe_w_down, final_norm_w):
    nb, seq, d = x.shape
    depth = w_in.shape[0]
    sg_heads = sg_w.shape[1]
    ret_heads = (w_in.shape[2] - 2 * sg_heads * SG_HEAD_DIM) // (4 * RET_HEAD_DIM)
    n = nb * seq
    assert seq % CHUNK == 0 and sg_w.shape[2] == CHUNK

    mod = _modulation(c, ada_w, ada_b).reshape(depth, nb, 6, d)
    cos, sin, dmask, xi, zeta, dec = _retention_tables(ret_heads, seq)
    xf = x.reshape(n, d)
    w_in_b, w_out_b = w_in.astype(BF16), w_out.astype(BF16)
    ffn_b = [w.astype(BF16) for w in (ffn_w_gate, ffn_w_up, ffn_w_down)]
    final_w = final_norm_w.reshape(1, d)
    for l in range(depth):
        z = _inproj(xf, norm1_w[l].reshape(1, d), mod[l], w_in_b, l, seq)
        sgb = jnp.broadcast_to(sg_b[l][:, :, None], (sg_heads, CHUNK, LANES))
        y = _mixer(z, cos, sin, sg_w[l], sgb, dmask, xi, zeta, dec, seq)
        i = l // 2
        nw2 = norm2_w[l].reshape(1, d)
        if l % 2 == 0:
            xf, h2 = _outproj(y, w_out_b, l, xf, mod[l], nw2, seq)
            xf = _ffn_dense(h2, *ffn_b, i, xf, mod[l], seq)
        else:
            n_experts = router_w.shape[2]
            tm = min(TM_FFN, n)
            xf, h2, route = _outproj(y, w_out_b, l, xf, mod[l], nw2, seq,
                                     router=(router_w[i], router_b[i]))
            tile_expert, n_used, tile_rows, src_token, p0, p1 = _routing_tables(route, n_experts, tm)
            o = _ffn_moe(tile_expert, n_used, tile_rows, src_token, h2, moe_w_gate, moe_w_up,
                         moe_w_down, i, tm)
            xf = _combine(o, p0, p1, route, xf, mod[l], seq,
                          final_w=final_w if l == depth - 1 else None)
    if depth % 2 == 1:
        xf = _final_norm(xf, final_w)
    return xf.reshape(nb, seq, d)
```

```python
import functools

import jax
import jax.numpy as jnp
from jax import lax
from jax.experimental import pallas as pl
from jax.experimental.pallas import tpu as pltpu

F32 = jnp.float32
BF16 = jnp.bfloat16

EPS = 1e-6
ROPE_BASE = 10000.0
SG_HEAD_DIM = 128
RET_HEAD_DIM = 256
CHUNK = 128
TOP_K = 2
LANES = 128

TM_INPROJ = 512
TC_MIXER = 512
MIXER_CHUNK_UNROLL = 2
TM_OUTPROJ = 512
ROW_CHUNKS_OUTPROJ = 2
TM_FFN = 1024
TF_FFN = 512
TN_DOWN_DENSE = 512
TN_DOWN_MOE = 256
ROW_BLOCK_MOE = 256
GATHER_ROWS_PER_STEP = 128
TR_COMBINE = 1024
TN_MOD = 512
DMA_ISSUE_UNROLL = 8
VMEM_LIMIT = 60 * 1024 * 1024


def _params(n_axes):
    return pltpu.CompilerParams(dimension_semantics=("arbitrary",) * n_axes,
                                vmem_limit_bytes=VMEM_LIMIT)


def _gelu_tanh(x):
    return 0.5 * x * (1.0 + jnp.tanh(0.7978845608028654 * (x + 0.044715 * (x * x * x))))


def _silu(x):
    return x * jax.nn.sigmoid(x)


def _pack_bf16_pairs(x):
    k = x.shape[1] // 2
    lo = lax.bitcast_convert_type(x[:, :k].astype(BF16).astype(F32), jnp.uint32)
    hi = lax.bitcast_convert_type(x[:, k:].astype(BF16).astype(F32), jnp.uint32)
    return (lo >> 16) | (hi & jnp.uint32(0xFFFF0000))


def _unpack_bf16_pairs(p):
    lo = lax.bitcast_convert_type(p << 16, F32)
    hi = lax.bitcast_convert_type(p & jnp.uint32(0xFFFF0000), F32)
    return jnp.concatenate([lo.astype(BF16), hi.astype(BF16)], axis=1)


def _rms_modulate(x, nw, shift, scale):
    y = x * lax.rsqrt(jnp.mean(x * x, axis=-1, keepdims=True) + EPS)
    return (y * nw) * (1.0 + scale) + shift


def _mod_kernel(cb_ref, w_ref, b_ref, o_ref):
    tn = w_ref.shape[1]
    for b in range(cb_ref.shape[0]):
        act = _silu(cb_ref[b])
        cols = [jnp.sum(w_ref[:, j * LANES:(j + 1) * LANES] * act, axis=0, keepdims=True)
                for j in range(tn // LANES)]
        o_ref[b:b + 1, :] = jnp.concatenate(cols, axis=1) + b_ref[...]


def _modulation(c, ada_w, ada_b):
    depth, d, n6 = ada_w.shape
    nb = c.shape[0]
    tn = min(TN_MOD, n6)
    cb = jnp.broadcast_to(c[:, :, None], (nb, d, LANES))
    return pl.pallas_call(
        _mod_kernel,
        out_shape=jax.ShapeDtypeStruct((depth, nb, n6), F32),
        grid=(depth, n6 // tn),
        in_specs=[pl.BlockSpec((nb, d, LANES), lambda l, j: (0, 0, 0)),
                  pl.BlockSpec((None, d, tn), lambda l, j: (l, 0, j)),
                  pl.BlockSpec((None, 1, tn), lambda l, j: (l, 0, j))],
        out_specs=pl.BlockSpec((None, nb, tn), lambda l, j: (l, 0, j)),
        compiler_params=_params(2),
        name="adaln_mod",
    )(cb, ada_w, ada_b.reshape(depth, 1, n6))


def _inproj_kernel(x_ref, nw_ref, mod_ref, w_ref, z_ref):
    h = _rms_modulate(x_ref[...], nw_ref[...], mod_ref[0:1, :], mod_ref[1:2, :])
    z_ref[...] = jnp.dot(h.astype(BF16), w_ref[...], preferred_element_type=F32).astype(z_ref.dtype)


def _inproj(x, nw, mod, w, layer, seq):
    n, d = x.shape
    nin = w.shape[2]
    tm = min(TM_INPROJ, seq)
    assert seq % tm == 0
    per_b = seq // tm
    return pl.pallas_call(
        _inproj_kernel,
        out_shape=jax.ShapeDtypeStruct((n, nin), BF16),
        grid=(n // tm,),
        in_specs=[pl.BlockSpec((tm, d), lambda i: (i, 0)),
                  pl.BlockSpec((1, d), lambda i: (0, 0)),
                  pl.BlockSpec((None, 6, d), lambda i: (i // per_b, 0, 0)),
                  pl.BlockSpec((None, d, nin), lambda i: (layer, 0, 0), pipeline_mode=pl.Buffered(1))],
        out_specs=pl.BlockSpec((tm, nin), lambda i: (i, 0)),
        compiler_params=_params(1),
        name="inproj",
    )(x, nw, mod, w)


def _mixer_kernel(z_ref, cos_ref, sin_ref, sgw_ref, sgb_ref, dmask_ref, xi_ref, zeta_ref,
                  dec_ref, y_ref, wm_ref, state_ref, *, sg_heads, ret_heads, steps_per_seq):
    sgw_w = sg_heads * SG_HEAD_DIM
    ret_w = ret_heads * RET_HEAD_DIM
    q_off, k_off = 2 * sgw_w, 2 * sgw_w + ret_w
    v_off, g_off = 2 * sgw_w + 2 * ret_w, 2 * sgw_w + 3 * ret_w
    half = RET_HEAD_DIM // 2
    step = pl.program_id(0)

    @pl.when(step == 0)
    def _():
        t = lax.broadcasted_iota(jnp.int32, (CHUNK, CHUNK), 0)
        s = lax.broadcasted_iota(jnp.int32, (CHUNK, CHUNK), 1)
        causal = (t >= s).astype(F32)
        for h in range(sg_heads):
            wm_ref[h] = (sgw_ref[h] * causal).astype(BF16)

    @pl.when(step % steps_per_seq == 0)
    def _():
        state_ref[...] = jnp.zeros_like(state_ref)

    def rotate(t, cos, sin):
        t1, t2 = t[:, :half], t[:, half:]
        return jnp.concatenate([t1 * cos - t2 * sin, t1 * sin + t2 * cos], axis=-1)

    def chunk(c, carry):
        rows = pl.ds(pl.multiple_of(c * CHUNK, CHUNK), CHUNK)
        cos = cos_ref[rows, :]
        sin = sin_ref[rows, :]
        for h in range(sg_heads):
            lo = h * SG_HEAD_DIM
            u = _gelu_tanh(z_ref[rows, lo:lo + SG_HEAD_DIM].astype(F32))
            v = _gelu_tanh(z_ref[rows, sgw_w + lo:sgw_w + lo + SG_HEAD_DIM].astype(F32))
            vc = v - jnp.mean(v, axis=-1, keepdims=True)
            vn = vc * lax.rsqrt(jnp.mean(vc * vc, axis=-1, keepdims=True) + EPS)
            sv = jnp.dot(wm_ref[h], vn.astype(BF16), preferred_element_type=F32) + sgb_ref[h]
            y_ref[rows, lo:lo + SG_HEAD_DIM] = (u * sv).astype(y_ref.dtype)
        for h in range(ret_heads):
            lo = h * RET_HEAD_DIM
            q = z_ref[rows, q_off + lo:q_off + lo + RET_HEAD_DIM].astype(F32)
            k = z_ref[rows, k_off + lo:k_off + lo + RET_HEAD_DIM].astype(F32)
            v = z_ref[rows, v_off + lo:v_off + lo + RET_HEAD_DIM]
            g = z_ref[rows, g_off + lo:g_off + lo + RET_HEAD_DIM].astype(F32)
            qr = rotate(q, cos, sin)
            kr = rotate(k, cos, sin) * (RET_HEAD_DIM ** -0.5)
            qb = qr.astype(BF16)
            scores = lax.dot_general(qb, kr.astype(BF16), (((1,), (1,)), ((), ())),
                                     preferred_element_type=F32) * dmask_ref[h]
            st = state_ref[h]
            yv = (jnp.dot(scores.astype(BF16), v, preferred_element_type=F32)
                  + jnp.dot(qb, st.astype(BF16), preferred_element_type=F32) * xi_ref[h])
            kz = (kr * zeta_ref[h]).astype(BF16)
            state_ref[h] = st * dec_ref[h] + lax.dot_general(
                kz, v, (((0,), (0,)), ((), ())), preferred_element_type=F32)
            yn = yv * lax.rsqrt(jnp.mean(yv * yv, axis=-1, keepdims=True) + EPS)
            y_ref[rows, sgw_w + lo:sgw_w + lo + RET_HEAD_DIM] = (_silu(g) * yn).astype(y_ref.dtype)
        return carry

    lax.fori_loop(0, z_ref.shape[0] // CHUNK, chunk, 0, unroll=MIXER_CHUNK_UNROLL)


def _mixer(z, cos, sin, sgw, sgb, dmask, xi, zeta, dec, seq):
    n, nin = z.shape
    sg_heads = sgw.shape[0]
    ret_heads = dmask.shape[0]
    mix = sg_heads * SG_HEAD_DIM + ret_heads * RET_HEAD_DIM
    tc = min(TC_MIXER, seq)
    per_seq = seq // tc
    const3 = lambda i: (0, 0, 0)
    kern = functools.partial(_mixer_kernel, sg_heads=sg_heads, ret_heads=ret_heads,
                             steps_per_seq=per_seq)
    return pl.pallas_call(
        kern,
        out_shape=jax.ShapeDtypeStruct((n, mix), BF16),
        grid=(n // tc,),
        in_specs=[pl.BlockSpec((tc, nin), lambda i: (i, 0)),
                  pl.BlockSpec((tc, LANES), lambda i: (i % per_seq, 0)),
                  pl.BlockSpec((tc, LANES), lambda i: (i % per_seq, 0)),
                  pl.BlockSpec(sgw.shape, const3),
                  pl.BlockSpec(sgb.shape, const3),
                  pl.BlockSpec(dmask.shape, const3),
                  pl.BlockSpec(xi.shape, const3),
                  pl.BlockSpec(zeta.shape, const3),
                  pl.BlockSpec(dec.shape, const3)],
        out_specs=pl.BlockSpec((tc, mix), lambda i: (i, 0)),
        scratch_shapes=[pltpu.VMEM((sg_heads, CHUNK, CHUNK), BF16),
                        pltpu.VMEM((ret_heads, RET_HEAD_DIM, RET_HEAD_DIM), F32)],
        compiler_params=_params(1),
        name="mixer",
    )(z, cos, sin, sgw, sgb, dmask, xi, zeta, dec)


def _route_top2(logits):
    lane = lax.broadcasted_iota(jnp.int32, logits.shape, 1)
    m1 = jnp.max(logits, axis=-1, keepdims=True)
    i1 = jnp.min(jnp.where(logits == m1, lane, LANES), axis=-1, keepdims=True)
    rest = jnp.where(lane == i1, -jnp.inf, logits)
    m2 = jnp.max(rest, axis=-1, keepdims=True)
    i2 = jnp.min(jnp.where(rest == m2, lane, LANES), axis=-1, keepdims=True)
    e = jnp.exp(m2 - m1)
    den = 1.0 + e
    w1 = 1.0 / den
    w2 = e / den
    return jnp.where(lane == 0, i1.astype(F32),
                     jnp.where(lane == 1, i2.astype(F32),
                               jnp.where(lane == 2, w1, jnp.where(lane == 3, w2, 0.0))))


def _router_logits(h2, rwt_ref, rb_ref, n_experts):
    lane = lax.broadcasted_iota(jnp.int32, (h2.shape[0], LANES), 1)
    logits = jnp.full((h2.shape[0], LANES), -jnp.inf, F32)
    for e in range(n_experts):
        s = jnp.sum(h2 * rwt_ref[e:e + 1, :], axis=-1, keepdims=True)
        logits = jnp.where(lane == e, s, logits)
    return logits + rb_ref[...]


def _outproj_kernel(*refs, n_experts, row_chunks):
    if n_experts:
        y_ref, w_ref, x_ref, mod_ref, nw_ref, rwt_ref, rb_ref, xo_ref, h2_ref, r_ref = refs
    else:
        y_ref, w_ref, x_ref, mod_ref, nw_ref, xo_ref, h2_ref = refs
    rc = y_ref.shape[0] // row_chunks
    for c in range(row_chunks):
        rows = slice(c * rc, (c + 1) * rc)
        yo = jnp.dot(y_ref[rows, :], w_ref[...], preferred_element_type=F32)
        xn = x_ref[rows, :] + mod_ref[2:3, :] * yo
        xo_ref[rows, :] = xn
        h2 = _rms_modulate(xn, nw_ref[...], mod_ref[3:4, :], mod_ref[4:5, :])
        if n_experts:
            h2_ref[rows, :] = _pack_bf16_pairs(h2)
            r_ref[rows, :] = _route_top2(_router_logits(h2, rwt_ref, rb_ref, n_experts))
        else:
            h2_ref[rows, :] = h2.astype(h2_ref.dtype)


def _outproj(y, w, layer, x, mod, nw, seq, router=None):
    n, d = x.shape
    mix = y.shape[1]
    tm = min(TM_OUTPROJ, seq)
    per_b = seq // tm
    row = lambda i: (i, 0)
    const = lambda i: (0, 0)
    in_specs = [pl.BlockSpec((tm, mix), row),
                pl.BlockSpec((None, mix, d), lambda i: (layer, 0, 0)),
                pl.BlockSpec((tm, d), row),
                pl.BlockSpec((None, 6, d), lambda i: (i // per_b, 0, 0)),
                pl.BlockSpec((1, d), const)]
    args = [y, w, x, mod, nw]
    out_shape = [jax.ShapeDtypeStruct((n, d), F32)]
    out_specs = [pl.BlockSpec((tm, d), row)]
    n_experts = 0
    if router is None:
        out_shape.append(jax.ShapeDtypeStruct((n, d), BF16))
        out_specs.append(pl.BlockSpec((tm, d), row))
    else:
        rw, rb = router
        n_experts = rw.shape[1]
        rb = jnp.pad(rb, (0, LANES - n_experts)).reshape(1, LANES)
        in_specs += [pl.BlockSpec((n_experts, d), const), pl.BlockSpec((1, LANES), const)]
        args += [rw.T, rb]
        out_shape += [jax.ShapeDtypeStruct((n, d // 2), jnp.uint32),
                      jax.ShapeDtypeStruct((n, LANES), F32)]
        out_specs += [pl.BlockSpec((tm, d // 2), row), pl.BlockSpec((tm, LANES), row)]
    return pl.pallas_call(
        functools.partial(_outproj_kernel, n_experts=n_experts, row_chunks=ROW_CHUNKS_OUTPROJ),
        out_shape=out_shape,
        grid=(n // tm,),
        in_specs=in_specs,
        out_specs=out_specs,
        compiler_params=_params(1),
        name="outproj_router" if n_experts else "outproj",
    )(*args)


def _ffn_up_step(h_ref, wg_ref, wu_ref, a_ref, j, tf, rows=None):
    rows = h_ref.shape[0] if rows is None else rows
    h = h_ref[:rows, :]
    g = jnp.dot(h, wg_ref[...].astype(BF16), preferred_element_type=F32)
    u = jnp.dot(h, wu_ref[...].astype(BF16), preferred_element_type=F32)
    a = (_silu(g) * u).astype(BF16)
    a_ref[:rows, pl.ds(pl.multiple_of(j * tf, tf), tf)] = a


def _ffn_down(a_ref, wd_ref, rows=None):
    rows = a_ref.shape[0] if rows is None else rows
    return jnp.dot(a_ref[:rows, :], wd_ref[...].astype(BF16), preferred_element_type=F32)


def _ffn_dense_kernel(h_ref, wg_ref, wu_ref, wd_ref, x_ref, mod_ref, o_ref, a_ref, *, nj, tf):
    j = pl.program_id(1)

    @pl.when(j < nj)
    def _():
        _ffn_up_step(h_ref, wg_ref, wu_ref, a_ref, j, tf)

    @pl.when(j >= nj)
    def _():
        o_ref[...] = x_ref[...] + mod_ref[5:6, :] * _ffn_down(a_ref, wd_ref)


def _ffn_dense(h2, wg, wu, wd, layer, x, mod, seq):
    n, d = x.shape
    f = wg.shape[2]
    tm = min(TM_FFN, seq)
    tf = min(TF_FFN, f)
    tn = min(TN_DOWN_DENSE, d)
    nj, nd = f // tf, d // tn
    per_b = seq // tm
    feat = lambda j: jnp.minimum(j, nj - 1)
    col = lambda j: jnp.maximum(j - nj, 0)
    return pl.pallas_call(
        functools.partial(_ffn_dense_kernel, nj=nj, tf=tf),
        out_shape=jax.ShapeDtypeStruct((n, d), F32),
        grid=(n // tm, nj + nd),
        in_specs=[pl.BlockSpec((tm, d), lambda i, j: (i, 0)),
                  pl.BlockSpec((None, d, tf), lambda i, j: (layer, 0, feat(j))),
                  pl.BlockSpec((None, d, tf), lambda i, j: (layer, 0, feat(j))),
                  pl.BlockSpec((None, f, tn), lambda i, j: (layer, 0, col(j))),
                  pl.BlockSpec((tm, tn), lambda i, j: (i, col(j))),
                  pl.BlockSpec((None, 6, tn), lambda i, j: (i // per_b, 0, col(j)))],
        out_specs=pl.BlockSpec((tm, tn), lambda i, j: (i, col(j))),
        scratch_shapes=[pltpu.VMEM((tm, f), BF16)],
        compiler_params=_params(2),
        name="ffn_dense",
    )(h2, wg, wu, wd, x, mod)


def _row_copy(src_hbm, dst_ref, src_row, dst_row, sem):
    return pltpu.make_async_copy(src_hbm.at[pl.ds(src_row, 1), :],
                                 dst_ref.at[pl.ds(dst_row, 1), :], sem)


def _ffn_moe_kernel(te_ref, nu_ref, nv_ref, src_ref, hp_hbm, wg_ref, wu_ref, wd_ref, o_ref,
                    a_ref, h_ref, stage_ref, sem, *, nj, tf):
    i = pl.program_id(0)
    j = pl.program_id(1)
    tm = h_ref.shape[0]
    n_used = nu_ref[0]
    active = i < n_used
    blocks = (nv_ref[i] + ROW_BLOCK_MOE - 1) // ROW_BLOCK_MOE

    def start_rows(tile, first, count):
        def body(k, carry):
            r = first + k
            _row_copy(hp_hbm, stage_ref, src_ref[tile * tm + r], r, sem).start()
            return carry
        lax.fori_loop(0, count, body, 0, unroll=DMA_ISSUE_UNROLL)

    @pl.when(jnp.logical_and(i == 0, j == 0))
    def _():
        start_rows(0, 0, tm)

    @pl.when(jnp.logical_and(active, j == 0))
    def _():
        pltpu.make_async_copy(hp_hbm.at[pl.ds(0, tm), :], stage_ref, sem).wait()
        h_ref[...] = _unpack_bf16_pairs(stage_ref[...])

    prefetching = jnp.logical_and(j >= 1, j <= tm // GATHER_ROWS_PER_STEP)

    @pl.when(jnp.logical_and(i + 1 < n_used, prefetching))
    def _():
        start_rows(i + 1, (j - 1) * GATHER_ROWS_PER_STEP, GATHER_ROWS_PER_STEP)

    for q in range(1, tm // ROW_BLOCK_MOE + 1):
        rows = q * ROW_BLOCK_MOE
        this = jnp.logical_and(active, blocks == q)

        @pl.when(jnp.logical_and(this, j < nj))
        def _():
            _ffn_up_step(h_ref, wg_ref, wu_ref, a_ref, j, tf, rows=rows)

        @pl.when(jnp.logical_and(this, j >= nj))
        def _():
            o_ref[:rows, :] = _ffn_down(a_ref, wd_ref, rows=rows)
            if rows < tm:
                o_ref[rows:, :] = jnp.zeros((tm - rows, o_ref.shape[1]), o_ref.dtype)

    @pl.when(jnp.logical_and(jnp.logical_not(active), j >= nj))
    def _():
        o_ref[...] = jnp.zeros_like(o_ref)


def _ffn_moe(tile_expert, n_used, tile_rows, src_token, hp, wg, wu, wd, layer, tm):
    r = src_token.shape[0]
    d, f = wg.shape[2:]
    tf = min(TF_FFN, f)
    tn = min(TN_DOWN_MOE, d)
    nj, nd = f // tf, d // tn
    assert tm % ROW_BLOCK_MOE == 0 and tm % GATHER_ROWS_PER_STEP == 0
    assert tm // GATHER_ROWS_PER_STEP <= nj + nd - 1

    def feat(i, j, nu):
        return jnp.where(i < nu[0], jnp.minimum(j, nj - 1), nj - 1)

    def col_in(i, j, nu):
        return jnp.where(i < nu[0], jnp.maximum(j - nj, 0), nd - 1)

    up_spec = pl.BlockSpec((None, None, d, tf),
                           lambda i, j, te, nu, nv, src: (layer, te[i], 0, feat(i, j, nu)))
    grid_spec = pltpu.PrefetchScalarGridSpec(
        num_scalar_prefetch=4,
        grid=(r // tm, nj + nd),
        in_specs=[pl.BlockSpec(memory_space=pl.ANY),
                  up_spec, up_spec,
                  pl.BlockSpec((None, None, f, tn),
                               lambda i, j, te, nu, nv, src: (layer, te[i], 0, col_in(i, j, nu)))],
        out_specs=pl.BlockSpec((tm, tn), lambda i, j, te, nu, nv, src: (i, jnp.maximum(j - nj, 0))),
        scratch_shapes=[pltpu.VMEM((tm, f), BF16), pltpu.VMEM((tm, d), BF16),
                        pltpu.VMEM((tm, d // 2), jnp.uint32), pltpu.SemaphoreType.DMA],
    )
    return pl.pallas_call(
        functools.partial(_ffn_moe_kernel, nj=nj, tf=tf),
        out_shape=jax.ShapeDtypeStruct((r, d), F32),
        grid_spec=grid_spec,
        compiler_params=_params(2),
        name="ffn_moe",
    )(tile_expert, n_used, tile_rows, src_token, hp, wg, wu, wd)


def _combine_kernel(*refs, final_norm):
    if final_norm:
        p0_ref, p1_ref, o_hbm, route_ref, x_ref, mod_ref, fw_ref, out_ref, buf0_ref, buf1_ref, sem = refs
    else:
        p0_ref, p1_ref, o_hbm, route_ref, x_ref, mod_ref, out_ref, buf0_ref, buf1_ref, sem = refs
    rows = buf0_ref.shape[0]

    def issue(k, carry):
        _row_copy(o_hbm, buf0_ref, p0_ref[0, k], k, sem).start()
        _row_copy(o_hbm, buf1_ref, p1_ref[0, k], k, sem).start()
        return carry

    lax.fori_loop(0, rows, issue, 0, unroll=DMA_ISSUE_UNROLL)
    pltpu.make_async_copy(o_hbm.at[pl.ds(0, rows), :], buf0_ref, sem).wait()
    pltpu.make_async_copy(o_hbm.at[pl.ds(0, rows), :], buf1_ref, sem).wait()
    w0 = route_ref[:, TOP_K:TOP_K + 1]
    w1 = route_ref[:, TOP_K + 1:TOP_K + 2]
    xn = x_ref[...] + mod_ref[5:6, :] * (w0 * buf0_ref[...] + w1 * buf1_ref[...])
    if final_norm:
        xn = (xn * lax.rsqrt(jnp.mean(xn * xn, axis=-1, keepdims=True) + EPS)) * fw_ref[...]
    out_ref[...] = xn


def _combine(o, p0, p1, route, x, mod, seq, final_w=None):
    n, d = x.shape
    tr = min(TR_COMBINE, seq)
    per_b = seq // tr
    idx_spec = pl.BlockSpec((None, 1, tr), lambda i: (i, 0, 0), memory_space=pltpu.SMEM)
    in_specs = [idx_spec, idx_spec,
                pl.BlockSpec(memory_space=pl.ANY),
                pl.BlockSpec((tr, LANES), lambda i: (i, 0)),
                pl.BlockSpec((tr, d), lambda i: (i, 0)),
                pl.BlockSpec((None, 6, d), lambda i: (i // per_b, 0, 0))]
    args = [p0.reshape(n // tr, 1, tr), p1.reshape(n // tr, 1, tr), o, route, x, mod]
    if final_w is not None:
        in_specs.append(pl.BlockSpec((1, d), lambda i: (0, 0)))
        args.append(final_w)
    return pl.pallas_call(
        functools.partial(_combine_kernel, final_norm=final_w is not None),
        out_shape=jax.ShapeDtypeStruct((n, d), F32),
        grid=(n // tr,),
        in_specs=in_specs,
        out_specs=pl.BlockSpec((tr, d), lambda i: (i, 0)),
        scratch_shapes=[pltpu.VMEM((tr, d), F32), pltpu.VMEM((tr, d), F32),
                        pltpu.SemaphoreType.DMA],
        compiler_params=_params(1),
        name="moe_combine",
    )(*args)


def _routing_tables(route, n_experts, tm):
    n = route.shape[0]
    i32 = jnp.int32
    ids = route[:, :TOP_K].astype(i32).reshape(-1)
    n_assign = n * TOP_K
    n_tiles = n_assign // tm + n_experts
    experts = jnp.arange(n_experts, dtype=i32)
    onehot = (ids[:, None] == experts[None, :]).astype(i32)
    ranks = jnp.cumsum(onehot, axis=0)
    counts = ranks[-1]
    offs = jnp.cumsum(counts) - counts
    tiles_per = (counts + tm - 1) // tm
    tile_end = jnp.cumsum(tiles_per)
    n_used = tile_end[-1]
    pad_offs = (tile_end - tiles_per) * tm
    pos = jnp.sum(onehot * (pad_offs[None, :] + ranks - 1), axis=1).reshape(n, TOP_K)
    tile_ids = jnp.arange(n_tiles, dtype=i32)
    tile_expert = jnp.sum((tile_ids[:, None] >= tile_end[None, :]).astype(i32), axis=1)
    last_expert = jnp.sum((n_used - 1 >= tile_end).astype(i32))
    tile_expert = jnp.where(tile_ids < n_used, tile_expert, last_expert).astype(i32)
    _, sorted_tok = lax.sort((ids, jnp.arange(n_assign, dtype=i32) // TOP_K), num_keys=1,
                             is_stable=True)
    row_expert = jnp.repeat(tile_expert, tm)
    rows = jnp.arange(n_tiles * tm, dtype=i32)
    shift = jnp.sum((row_expert[:, None] == experts[None, :]) * (offs - pad_offs)[None, :], axis=1)
    limit = jnp.sum((row_expert[:, None] == experts[None, :]) * (offs + counts)[None, :], axis=1)
    src = rows + shift
    valid = (src < limit) & (rows < n_used * tm)
    src_token = jnp.where(valid, sorted_tok[jnp.clip(src, 0, n_assign - 1)], 0).astype(i32)
    tile_rows = jnp.sum(valid.reshape(n_tiles, tm).astype(i32), axis=1)
    return (tile_expert, n_used.reshape(1).astype(i32), tile_rows, src_token,
            pos[:, 0].astype(i32), pos[:, 1].astype(i32))


def _final_norm_kernel(x_ref, w_ref, o_ref):
    x = x_ref[...]
    o_ref[...] = (x * lax.rsqrt(jnp.mean(x * x, axis=-1, keepdims=True) + EPS)) * w_ref[...]


def _final_norm(x, w):
    n, d = x.shape
    tm = min(TM_FFN, n)
    return pl.pallas_call(
        _final_norm_kernel,
        out_shape=jax.ShapeDtypeStruct((n, d), F32),
        grid=(n // tm,),
        in_specs=[pl.BlockSpec((tm, d), lambda i: (i, 0)), pl.BlockSpec((1, d), lambda i: (0, 0))],
        out_specs=pl.BlockSpec((tm, d), lambda i: (i, 0)),
        compiler_params=_params(1),
        name="final_norm",
    )(x, w)


def _retention_tables(ret_heads, seq):
    half = RET_HEAD_DIM // 2
    inv_freq = 1.0 / (ROPE_BASE ** (jnp.arange(half, dtype=F32) * 2.0 / RET_HEAD_DIM))
    ang = jnp.arange(seq, dtype=F32)[:, None] * inv_freq[None, :]
    log_gamma = jnp.log(1.0 - 2.0 ** (-5.0 - jnp.arange(ret_heads, dtype=F32)))
    j = jnp.arange(CHUNK, dtype=F32)
    diff = j[:, None] - j[None, :]
    dmask = jnp.where(diff >= 0, jnp.exp(log_gamma[:, None, None] * jnp.maximum(diff, 0.0)), 0.0)
    xi = jnp.exp(log_gamma[:, None] * (j[None, :] + 1.0))
    zeta = jnp.exp(log_gamma[:, None] * (CHUNK - 1.0 - j[None, :]))
    dec = jnp.exp(log_gamma * CHUNK)
    wide = (ret_heads, CHUNK, RET_HEAD_DIM)
    return (jnp.cos(ang), jnp.sin(ang), dmask,
            jnp.broadcast_to(xi[:, :, None], wide), jnp.broadcast_to(zeta[:, :, None], wide),
            jnp.broadcast_to(dec[:, None, None], (ret_heads, 1, RET_HEAD_DIM)))


def kernel(x, c, norm1_w, ada_w, ada_b, w_in, sg_w, sg_b, w_out, norm2_w, ffn_w_gate, ffn_w_up, ffn_w_down, router_w, router_b, moe_w_gate, moe_w_up, moe_w_down, final_norm_w):
    nb, seq, d = x.shape
    depth = w_in.shape[0]
    sg_heads = sg_w.shape[1]
    ret_heads = (w_in.shape[2] - 2 * sg_heads * SG_HEAD_DIM) // (4 * RET_HEAD_DIM)
    n = nb * seq
    assert seq % CHUNK == 0 and sg_w.shape[2] == CHUNK

    mod = _modulation(c, ada_w, ada_b).reshape(depth, nb, 6, d)
    cos, sin, dmask, xi, zeta, dec = _retention_tables(ret_heads, seq)
    xf = x.reshape(n, d)
    w_in_b, w_out_b = w_in.astype(BF16), w_out.astype(BF16)
    ffn_b = [w.astype(BF16) for w in (ffn_w_gate, ffn_w_up, ffn_w_down)]
    final_w = final_norm_w.reshape(1, d)
    for l in range(depth):
        z = _inproj(xf, norm1_w[l].reshape(1, d), mod[l], w_in_b, l, seq)
        sgb = jnp.broadcast_to(sg_b[l][:, :, None], (sg_heads, CHUNK, LANES))
        y = _mixer(z, cos, sin, sg_w[l], sgb, dmask, xi, zeta, dec, seq)
        i = l // 2
        nw2 = norm2_w[l].reshape(1, d)
        if l % 2 == 0:
            xf, h2 = _outproj(y, w_out_b, l, xf, mod[l], nw2, seq)
            xf = _ffn_dense(h2, *ffn_b, i, xf, mod[l], seq)
        else:
            n_experts = router_w.shape[2]
            tm = min(TM_FFN, n)
            xf, h2, route = _outproj(y, w_out_b, l, xf, mod[l], nw2, seq,
                                     router=(router_w[i], router_b[i]))
            tile_expert, n_used, tile_rows, src_token, p0, p1 = _routing_tables(route, n_experts, tm)
            o = _ffn_moe(tile_expert, n_used, tile_rows, src_token, h2, moe_w_gate, moe_w_up,
                         moe_w_down, i, tm)
            xf = _combine(o, p0, p1, route, xf, mod[l], seq,
                          final_w=final_w if l == depth - 1 else None)
    if depth % 2 == 1:
        xf = _final_norm(xf, final_w)
    return xf.reshape(nb, seq, d)
```

```python
import functools

import jax
import jax.numpy as jnp
from jax import lax
from jax.experimental import pallas as pl
from jax.experimental.pallas import tpu as pltpu

F32 = jnp.float32
BF16 = jnp.bfloat16

EPS = 1e-6
ROPE_BASE = 10000.0
SG_HEAD_DIM = 128
RET_HEAD_DIM = 256
CHUNK = 128
TOP_K = 2
LANES = 128

TM_INPROJ = 512
TC_MIXER = 512
MIXER_CHUNK_UNROLL = 2
TM_OUTPROJ = 512
ROW_CHUNKS_OUTPROJ = 2
TM_FFN = 1024
TF_FFN = 512
TN_DOWN_DENSE = 512
TN_DOWN_MOE = 256
ROW_BLOCK_MOE = 256
GATHER_ROWS_PER_STEP = 128
TR_COMBINE = 1024
TN_MOD = 512
DMA_ISSUE_UNROLL = 8
VMEM_LIMIT = 60 * 1024 * 1024


def _params(n_axes):
    return pltpu.CompilerParams(dimension_semantics=("arbitrary",) * n_axes,
                                vmem_limit_bytes=VMEM_LIMIT)


def _gelu_tanh(x):
    return 0.5 * x * (1.0 + jnp.tanh(0.7978845608028654 * (x + 0.044715 * (x * x * x))))


def _silu(x):
    return x * jax.nn.sigmoid(x)


def _pack_bf16_pairs(x):
    k = x.shape[1] // 2
    lo = lax.bitcast_convert_type(x[:, :k].astype(BF16).astype(F32), jnp.uint32)
    hi = lax.bitcast_convert_type(x[:, k:].astype(BF16).astype(F32), jnp.uint32)
    return (lo >> 16) | (hi & jnp.uint32(0xFFFF0000))


def _unpack_bf16_pairs(p):
    lo = lax.bitcast_convert_type(p << 16, F32)
    hi = lax.bitcast_convert_type(p & jnp.uint32(0xFFFF0000), F32)
    return jnp.concatenate([lo.astype(BF16), hi.astype(BF16)], axis=1)


def _rms_modulate(x, nw, shift, scale):
    y = x * lax.rsqrt(jnp.mean(x * x, axis=-1, keepdims=True) + EPS)
    return (y * nw) * (1.0 + scale) + shift


def _mod_kernel(cb_ref, w_ref, b_ref, o_ref):
    tn = w_ref.shape[1]
    for b in range(cb_ref.shape[0]):
        act = _silu(cb_ref[b])
        cols = [jnp.sum(w_ref[:, j * LANES:(j + 1) * LANES] * act, axis=0, keepdims=True)
                for j in range(tn // LANES)]
        o_ref[b:b + 1, :] = jnp.concatenate(cols, axis=1) + b_ref[...]


def _modulation(c, ada_w, ada_b):
    depth, d, n6 = ada_w.shape
    nb = c.shape[0]
    tn = min(TN_MOD, n6)
    cb = jnp.broadcast_to(c[:, :, None], (nb, d, LANES))
    return pl.pallas_call(
        _mod_kernel,
        out_shape=jax.ShapeDtypeStruct((depth, nb, n6), F32),
        grid=(depth, n6 // tn),
        in_specs=[pl.BlockSpec((nb, d, LANES), lambda l, j: (0, 0, 0)),
                  pl.BlockSpec((None, d, tn), lambda l, j: (l, 0, j)),
                  pl.BlockSpec((None, 1, tn), lambda l, j: (l, 0, j))],
        out_specs=pl.BlockSpec((None, nb, tn), lambda l, j: (l, 0, j)),
        compiler_params=_params(2),
        name="adaln_mod",
    )(cb, ada_w, ada_b.reshape(depth, 1, n6))


def _inproj_kernel(x_ref, nw_ref, mod_ref, w_ref, z_ref):
    h = _rms_modulate(x_ref[...], nw_ref[...], mod_ref[0:1, :], mod_ref[1:2, :])
    z_ref[...] = jnp.dot(h.astype(BF16), w_ref[...], preferred_element_type=F32).astype(z_ref.dtype)


def _inproj(x, nw, mod, w, layer, seq):
    n, d = x.shape
    nin = w.shape[2]
    tm = min(TM_INPROJ, seq)
    assert seq % tm == 0
    per_b = seq // tm
    return pl.pallas_call(
        _inproj_kernel,
        out_shape=jax.ShapeDtypeStruct((n, nin), BF16),
        grid=(n // tm,),
        in_specs=[pl.BlockSpec((tm, d), lambda i: (i, 0)),
                  pl.BlockSpec((1, d), lambda i: (0, 0)),
                  pl.BlockSpec((None, 6, d), lambda i: (i // per_b, 0, 0)),
                  pl.BlockSpec((None, d, nin), lambda i: (layer, 0, 0), pipeline_mode=pl.Buffered(1))],
        out_specs=pl.BlockSpec((tm, nin), lambda i: (i, 0)),
        compiler_params=_params(1),
        name="inproj",
    )(x, nw, mod, w)


def _mixer_kernel(z_ref, cos_ref, sin_ref, sgw_ref, sgb_ref, dmask_ref, xi_ref, zeta_ref,
                  dec_ref, y_ref, wm_ref, state_ref, *, sg_heads, ret_heads, steps_per_seq):
    sgw_w = sg_heads * SG_HEAD_DIM
    ret_w = ret_heads * RET_HEAD_DIM
    q_off, k_off = 2 * sgw_w, 2 * sgw_w + ret_w
    v_off, g_off = 2 * sgw_w + 2 * ret_w, 2 * sgw_w + 3 * ret_w
    half = RET_HEAD_DIM // 2
    step = pl.program_id(0)

    @pl.when(step == 0)
    def _():
        t = lax.broadcasted_iota(jnp.int32, (CHUNK, CHUNK), 0)
        s = lax.broadcasted_iota(jnp.int32, (CHUNK, CHUNK), 1)
        causal = (t >= s).astype(F32)
        for h in range(sg_heads):
            wm_ref[h] = (sgw_ref[h] * causal).astype(BF16)

    @pl.when(step % steps_per_seq == 0)
    def _():
        state_ref[...] = jnp.zeros_like(state_ref)

    def rotate(t, cos, sin):
        t1, t2 = t[:, :half], t[:, half:]
        return jnp.concatenate([t1 * cos - t2 * sin, t1 * sin + t2 * cos], axis=-1)

    def chunk(c, carry):
        rows = pl.ds(pl.multiple_of(c * CHUNK, CHUNK), CHUNK)
        cos = cos_ref[rows, :]
        sin = sin_ref[rows, :]
        for h in range(sg_heads):
            lo = h * SG_HEAD_DIM
            u = _gelu_tanh(z_ref[rows, lo:lo + SG_HEAD_DIM].astype(F32))
            v = _gelu_tanh(z_ref[rows, sgw_w + lo:sgw_w + lo + SG_HEAD_DIM].astype(F32))
            vc = v - jnp.mean(v, axis=-1, keepdims=True)
            vn = vc * lax.rsqrt(jnp.mean(vc * vc, axis=-1, keepdims=True) + EPS)
            sv = jnp.dot(wm_ref[h], vn.astype(BF16), preferred_element_type=F32) + sgb_ref[h]
            y_ref[rows, lo:lo + SG_HEAD_DIM] = (u * sv).astype(y_ref.dtype)
        for h in range(ret_heads):
            lo = h * RET_HEAD_DIM
            q = z_ref[rows, q_off + lo:q_off + lo + RET_HEAD_DIM].astype(F32)
            k = z_ref[rows, k_off + lo:k_off + lo + RET_HEAD_DIM].astype(F32)
            v = z_ref[rows, v_off + lo:v_off + lo + RET_HEAD_DIM]
            g = z_ref[rows, g_off + lo:g_off + lo + RET_HEAD_DIM].astype(F32)
            qr = rotate(q, cos, sin)
            kr = rotate(k, cos, sin) * (RET_HEAD_DIM ** -0.5)
            qb = qr.astype(BF16)
            scores = lax.dot_general(qb, kr.astype(BF16), (((1,), (1,)), ((), ())),
                                     preferred_element_type=F32) * dmask_ref[h]
            st = state_ref[h]
            yv = (jnp.dot(scores.astype(BF16), v, preferred_element_type=F32)
                  + jnp.dot(qb, st.astype(BF16), preferred_element_type=F32) * xi_ref[h])
            kz = (kr * zeta_ref[h]).astype(BF16)
            state_ref[h] = st * dec_ref[h] + lax.dot_general(
                kz, v, (((0,), (0,)), ((), ())), preferred_element_type=F32)
            yn = yv * lax.rsqrt(jnp.mean(yv * yv, axis=-1, keepdims=True) + EPS)
            y_ref[rows, sgw_w + lo:sgw_w + lo + RET_HEAD_DIM] = (_silu(g) * yn).astype(y_ref.dtype)
        return carry

    lax.fori_loop(0, z_ref.shape[0] // CHUNK, chunk, 0, unroll=MIXER_CHUNK_UNROLL)


def _mixer(z, cos, sin, sgw, sgb, dmask, xi, zeta, dec, seq):
    n, nin = z.shape
    sg_heads = sgw.shape[0]
    ret_heads = dmask.shape[0]
    mix = sg_heads * SG_HEAD_DIM + ret_heads * RET_HEAD_DIM
    tc = min(TC_MIXER, seq)
    per_seq = seq // tc
    const3 = lambda i: (0, 0, 0)
    kern = functools.partial(_mixer_kernel, sg_heads=sg_heads, ret_heads=ret_heads,
                             steps_per_seq=per_seq)
    return pl.pallas_call(
        kern,
        out_shape=jax.ShapeDtypeStruct((n, mix), BF16),
        grid=(n // tc,),
        in_specs=[pl.BlockSpec((tc, nin), lambda i: (i, 0)),
                  pl.BlockSpec((tc, LANES), lambda i: (i % per_seq, 0)),
                  pl.BlockSpec((tc, LANES), lambda i: (i % per_seq, 0)),
                  pl.BlockSpec(sgw.shape, const3),
                  pl.BlockSpec(sgb.shape, const3),
                  pl.BlockSpec(dmask.shape, const3),
                  pl.BlockSpec(xi.shape, const3),
                  pl.BlockSpec(zeta.shape, const3),
                  pl.BlockSpec(dec.shape, const3)],
        out_specs=pl.BlockSpec((tc, mix), lambda i: (i, 0)),
        scratch_shapes=[pltpu.VMEM((sg_heads, CHUNK, CHUNK), BF16),
                        pltpu.VMEM((ret_heads, RET_HEAD_DIM, RET_HEAD_DIM), F32)],
        compiler_params=_params(1),
        name="mixer",
    )(z, cos, sin, sgw, sgb, dmask, xi, zeta, dec)


def _route_top2(logits):
    lane = lax.broadcasted_iota(jnp.int32, logits.shape, 1)
    m1 = jnp.max(logits, axis=-1, keepdims=True)
    i1 = jnp.min(jnp.where(logits == m1, lane, LANES), axis=-1, keepdims=True)
    rest = jnp.where(lane == i1, -jnp.inf, logits)
    m2 = jnp.max(rest, axis=-1, keepdims=True)
    i2 = jnp.min(jnp.where(rest == m2, lane, LANES), axis=-1, keepdims=True)
    e = jnp.exp(m2 - m1)
    den = 1.0 + e
    w1 = 1.0 / den
    w2 = e / den
    return jnp.where(lane == 0, i1.astype(F32),
                     jnp.where(lane == 1, i2.astype(F32),
                               jnp.where(lane == 2, w1, jnp.where(lane == 3, w2, 0.0))))


def _router_logits(h2, rwt_ref, rb_ref, n_experts):
    lane = lax.broadcasted_iota(jnp.int32, (h2.shape[0], LANES), 1)
    logits = jnp.full((h2.shape[0], LANES), -jnp.inf, F32)
    for e in range(n_experts):
        s = jnp.sum(h2 * rwt_ref[e:e + 1, :], axis=-1, keepdims=True)
        logits = jnp.where(lane == e, s, logits)
    return logits + rb_ref[...]


def _outproj_kernel(*refs, n_experts, row_chunks):
    if n_experts:
        y_ref, w_ref, x_ref, mod_ref, nw_ref, rwt_ref, rb_ref, xo_ref, h2_ref, r_ref = refs
    else:
        y_ref, w_ref, x_ref, mod_ref, nw_ref, xo_ref, h2_ref = refs
    rc = y_ref.shape[0] // row_chunks
    for c in range(row_chunks):
        rows = slice(c * rc, (c + 1) * rc)
        yo = jnp.dot(y_ref[rows, :], w_ref[...], preferred_element_type=F32)
        xn = x_ref[rows, :] + mod_ref[2:3, :] * yo
        xo_ref[rows, :] = xn
        h2 = _rms_modulate(xn, nw_ref[...], mod_ref[3:4, :], mod_ref[4:5, :])
        if n_experts:
            h2_ref[rows] = _pack_bf16_pairs(h2).reshape(rc, -1, LANES)
            r_ref[rows, :] = _route_top2(_router_logits(h2, rwt_ref, rb_ref, n_experts))
        else:
            h2_ref[rows, :] = h2.astype(h2_ref.dtype)


def _outproj(y, w, layer, x, mod, nw, seq, router=None):
    n, d = x.shape
    mix = y.shape[1]
    tm = min(TM_OUTPROJ, seq)
    per_b = seq // tm
    row = lambda i: (i, 0)
    const = lambda i: (0, 0)
    in_specs = [pl.BlockSpec((tm, mix), row),
                pl.BlockSpec((None, mix, d), lambda i: (layer, 0, 0)),
                pl.BlockSpec((tm, d), row),
                pl.BlockSpec((None, 6, d), lambda i: (i // per_b, 0, 0)),
                pl.BlockSpec((1, d), const)]
    args = [y, w, x, mod, nw]
    out_shape = [jax.ShapeDtypeStruct((n, d), F32)]
    out_specs = [pl.BlockSpec((tm, d), row)]
    n_experts = 0
    if router is None:
        out_shape.append(jax.ShapeDtypeStruct((n, d), BF16))
        out_specs.append(pl.BlockSpec((tm, d), row))
    else:
        rw, rb = router
        n_experts = rw.shape[1]
        rb = jnp.pad(rb, (0, LANES - n_experts)).reshape(1, LANES)
        in_specs += [pl.BlockSpec((n_experts, d), const), pl.BlockSpec((1, LANES), const)]
        args += [rw.T, rb]
        slab = (d // 2 // LANES, LANES)
        out_shape += [jax.ShapeDtypeStruct((n, *slab), jnp.uint32),
                      jax.ShapeDtypeStruct((n, LANES), F32)]
        out_specs += [pl.BlockSpec((tm, *slab), lambda i: (i, 0, 0)),
                      pl.BlockSpec((tm, LANES), row)]
    return pl.pallas_call(
        functools.partial(_outproj_kernel, n_experts=n_experts, row_chunks=ROW_CHUNKS_OUTPROJ),
        out_shape=out_shape,
        grid=(n // tm,),
        in_specs=in_specs,
        out_specs=out_specs,
        compiler_params=_params(1),
        name="outproj_router" if n_experts else "outproj",
    )(*args)


def _ffn_up_step(h_ref, wg_ref, wu_ref, a_ref, j, tf, rows=None):
    rows = h_ref.shape[0] if rows is None else rows
    h = h_ref[:rows, :]
    g = jnp.dot(h, wg_ref[...].astype(BF16), preferred_element_type=F32)
    u = jnp.dot(h, wu_ref[...].astype(BF16), preferred_element_type=F32)
    a = (_silu(g) * u).astype(BF16)
    a_ref[:rows, pl.ds(pl.multiple_of(j * tf, tf), tf)] = a


def _ffn_down(a_ref, wd_ref, rows=None):
    rows = a_ref.shape[0] if rows is None else rows
    return jnp.dot(a_ref[:rows, :], wd_ref[...].astype(BF16), preferred_element_type=F32)


def _ffn_dense_kernel(h_ref, wg_ref, wu_ref, wd_ref, x_ref, mod_ref, o_ref, a_ref, *, nj, tf):
    j = pl.program_id(1)

    @pl.when(j < nj)
    def _():
        _ffn_up_step(h_ref, wg_ref, wu_ref, a_ref, j, tf)

    @pl.when(j >= nj)
    def _():
        o_ref[...] = x_ref[...] + mod_ref[5:6, :] * _ffn_down(a_ref, wd_ref)


def _ffn_dense(h2, wg, wu, wd, layer, x, mod, seq):
    n, d = x.shape
    f = wg.shape[2]
    tm = min(TM_FFN, seq)
    tf = min(TF_FFN, f)
    tn = min(TN_DOWN_DENSE, d)
    nj, nd = f // tf, d // tn
    per_b = seq // tm
    feat = lambda j: jnp.minimum(j, nj - 1)
    col = lambda j: jnp.maximum(j - nj, 0)
    return pl.pallas_call(
        functools.partial(_ffn_dense_kernel, nj=nj, tf=tf),
        out_shape=jax.ShapeDtypeStruct((n, d), F32),
        grid=(n // tm, nj + nd),
        in_specs=[pl.BlockSpec((tm, d), lambda i, j: (i, 0)),
                  pl.BlockSpec((None, d, tf), lambda i, j: (layer, 0, feat(j))),
                  pl.BlockSpec((None, d, tf), lambda i, j: (layer, 0, feat(j))),
                  pl.BlockSpec((None, f, tn), lambda i, j: (layer, 0, col(j))),
                  pl.BlockSpec((tm, tn), lambda i, j: (i, col(j))),
                  pl.BlockSpec((None, 6, tn), lambda i, j: (i // per_b, 0, col(j)))],
        out_specs=pl.BlockSpec((tm, tn), lambda i, j: (i, col(j))),
        scratch_shapes=[pltpu.VMEM((tm, f), BF16)],
        compiler_params=_params(2),
        name="ffn_dense",
    )(h2, wg, wu, wd, x, mod)


def _row_copy(src_hbm, dst_ref, src_row, dst_row, sem):
    return pltpu.make_async_copy(src_hbm.at[pl.ds(src_row, 1), :],
                                 dst_ref.at[pl.ds(dst_row, 1), :], sem)


def _ffn_moe_kernel(te_ref, nu_ref, nv_ref, src_ref, hp_hbm, wg_ref, wu_ref, wd_ref, o_ref,
                    a_ref, h_ref, stage_ref, sem, *, nj, tf):
    i = pl.program_id(0)
    j = pl.program_id(1)
    tm = h_ref.shape[0]
    n_used = nu_ref[0]
    active = i < n_used
    blocks = (nv_ref[i] + ROW_BLOCK_MOE - 1) // ROW_BLOCK_MOE

    def start_rows(tile, first, count):
        def body(k, carry):
            r = first + k
            pltpu.make_async_copy(hp_hbm.at[src_ref[tile * tm + r]], stage_ref.at[r], sem).start()
            return carry
        lax.fori_loop(0, count, body, 0, unroll=DMA_ISSUE_UNROLL)

    @pl.when(jnp.logical_and(i == 0, j == 0))
    def _():
        start_rows(0, 0, tm)

    @pl.when(jnp.logical_and(active, j == 0))
    def _():
        pltpu.make_async_copy(hp_hbm.at[pl.ds(0, tm)], stage_ref, sem).wait()
        h_ref[...] = _unpack_bf16_pairs(stage_ref[...].reshape(tm, -1))

    prefetching = jnp.logical_and(j >= 1, j <= tm // GATHER_ROWS_PER_STEP)

    @pl.when(jnp.logical_and(i + 1 < n_used, prefetching))
    def _():
        start_rows(i + 1, (j - 1) * GATHER_ROWS_PER_STEP, GATHER_ROWS_PER_STEP)

    for q in range(1, tm // ROW_BLOCK_MOE + 1):
        rows = q * ROW_BLOCK_MOE
        this = jnp.logical_and(active, blocks == q)

        @pl.when(jnp.logical_and(this, j < nj))
        def _():
            _ffn_up_step(h_ref, wg_ref, wu_ref, a_ref, j, tf, rows=rows)

        @pl.when(jnp.logical_and(this, j >= nj))
        def _():
            o_ref[:rows, :] = _ffn_down(a_ref, wd_ref, rows=rows)
            if rows < tm:
                o_ref[rows:, :] = jnp.zeros((tm - rows, o_ref.shape[1]), o_ref.dtype)

    @pl.when(jnp.logical_and(jnp.logical_not(active), j >= nj))
    def _():
        o_ref[...] = jnp.zeros_like(o_ref)


def _ffn_moe(tile_expert, n_used, tile_rows, src_token, hp, wg, wu, wd, layer, tm):
    r = src_token.shape[0]
    d, f = wg.shape[2:]
    tf = min(TF_FFN, f)
    tn = min(TN_DOWN_MOE, d)
    nj, nd = f // tf, d // tn
    assert tm % ROW_BLOCK_MOE == 0 and tm % GATHER_ROWS_PER_STEP == 0
    assert tm // GATHER_ROWS_PER_STEP <= nj + nd - 1

    def feat(i, j, nu):
        return jnp.where(i < nu[0], jnp.minimum(j, nj - 1), nj - 1)

    def col_in(i, j, nu):
        return jnp.where(i < nu[0], jnp.maximum(j - nj, 0), nd - 1)

    up_spec = pl.BlockSpec((None, None, d, tf),
                           lambda i, j, te, nu, nv, src: (layer, te[i], 0, feat(i, j, nu)))
    grid_spec = pltpu.PrefetchScalarGridSpec(
        num_scalar_prefetch=4,
        grid=(r // tm, nj + nd),
        in_specs=[pl.BlockSpec(memory_space=pl.ANY),
                  up_spec, up_spec,
                  pl.BlockSpec((None, None, f, tn),
                               lambda i, j, te, nu, nv, src: (layer, te[i], 0, col_in(i, j, nu)))],
        out_specs=pl.BlockSpec((tm, tn), lambda i, j, te, nu, nv, src: (i, jnp.maximum(j - nj, 0))),
        scratch_shapes=[pltpu.VMEM((tm, f), BF16), pltpu.VMEM((tm, d), BF16),
                        pltpu.VMEM((tm, *hp.shape[1:]), jnp.uint32), pltpu.SemaphoreType.DMA],
    )
    return pl.pallas_call(
        functools.partial(_ffn_moe_kernel, nj=nj, tf=tf),
        out_shape=jax.ShapeDtypeStruct((r, d), F32),
        grid_spec=grid_spec,
        compiler_params=_params(2),
        name="ffn_moe",
    )(tile_expert, n_used, tile_rows, src_token, hp, wg, wu, wd)


def _combine_kernel(*refs, final_norm):
    if final_norm:
        p0_ref, p1_ref, o_hbm, route_ref, x_ref, mod_ref, fw_ref, out_ref, buf0_ref, buf1_ref, sem = refs
    else:
        p0_ref, p1_ref, o_hbm, route_ref, x_ref, mod_ref, out_ref, buf0_ref, buf1_ref, sem = refs
    rows = buf0_ref.shape[0]

    def issue(k, carry):
        _row_copy(o_hbm, buf0_ref, p0_ref[0, k], k, sem).start()
        _row_copy(o_hbm, buf1_ref, p1_ref[0, k], k, sem).start()
        return carry

    lax.fori_loop(0, rows, issue, 0, unroll=DMA_ISSUE_UNROLL)
    pltpu.make_async_copy(o_hbm.at[pl.ds(0, rows), :], buf0_ref, sem).wait()
    pltpu.make_async_copy(o_hbm.at[pl.ds(0, rows), :], buf1_ref, sem).wait()
    w0 = route_ref[:, TOP_K:TOP_K + 1]
    w1 = route_ref[:, TOP_K + 1:TOP_K + 2]
    xn = x_ref[...] + mod_ref[5:6, :] * (w0 * buf0_ref[...] + w1 * buf1_ref[...])
    if final_norm:
        xn = (xn * lax.rsqrt(jnp.mean(xn * xn, axis=-1, keepdims=True) + EPS)) * fw_ref[...]
    out_ref[...] = xn


def _combine(o, p0, p1, route, x, mod, seq, final_w=None):
    n, d = x.shape
    tr = min(TR_COMBINE, seq)
    per_b = seq // tr
    idx_spec = pl.BlockSpec((None, 1, tr), lambda i: (i, 0, 0), memory_space=pltpu.SMEM)
    in_specs = [idx_spec, idx_spec,
                pl.BlockSpec(memory_space=pl.ANY),
                pl.BlockSpec((tr, LANES), lambda i: (i, 0)),
                pl.BlockSpec((tr, d), lambda i: (i, 0)),
                pl.BlockSpec((None, 6, d), lambda i: (i // per_b, 0, 0))]
    args = [p0.reshape(n // tr, 1, tr), p1.reshape(n // tr, 1, tr), o, route, x, mod]
    if final_w is not None:
        in_specs.append(pl.BlockSpec((1, d), lambda i: (0, 0)))
        args.append(final_w)
    return pl.pallas_call(
        functools.partial(_combine_kernel, final_norm=final_w is not None),
        out_shape=jax.ShapeDtypeStruct((n, d), F32),
        grid=(n // tr,),
        in_specs=in_specs,
        out_specs=pl.BlockSpec((tr, d), lambda i: (i, 0)),
        scratch_shapes=[pltpu.VMEM((tr, d), F32), pltpu.VMEM((tr, d), F32),
                        pltpu.SemaphoreType.DMA],
        compiler_params=_params(1),
        name="moe_combine",
    )(*args)


def _routing_tables(route, n_experts, tm):
    n = route.shape[0]
    i32 = jnp.int32
    ids = route[:, :TOP_K].astype(i32).reshape(-1)
    n_assign = n * TOP_K
    n_tiles = n_assign // tm + n_experts
    experts = jnp.arange(n_experts, dtype=i32)
    onehot = (ids[:, None] == experts[None, :]).astype(i32)
    ranks = jnp.cumsum(onehot, axis=0)
    counts = ranks[-1]
    offs = jnp.cumsum(counts) - counts
    tiles_per = (counts + tm - 1) // tm
    tile_end = jnp.cumsum(tiles_per)
    n_used = tile_end[-1]
    pad_offs = (tile_end - tiles_per) * tm
    pos = jnp.sum(onehot * (pad_offs[None, :] + ranks - 1), axis=1).reshape(n, TOP_K)
    tile_ids = jnp.arange(n_tiles, dtype=i32)
    tile_expert = jnp.sum((tile_ids[:, None] >= tile_end[None, :]).astype(i32), axis=1)
    last_expert = jnp.sum((n_used - 1 >= tile_end).astype(i32))
    tile_expert = jnp.where(tile_ids < n_used, tile_expert, last_expert).astype(i32)
    _, sorted_tok = lax.sort((ids, jnp.arange(n_assign, dtype=i32) // TOP_K), num_keys=1,
                             is_stable=True)
    row_expert = jnp.repeat(tile_expert, tm)
    rows = jnp.arange(n_tiles * tm, dtype=i32)
    shift = jnp.sum((row_expert[:, None] == experts[None, :]) * (offs - pad_offs)[None, :], axis=1)
    limit = jnp.sum((row_expert[:, None] == experts[None, :]) * (offs + counts)[None, :], axis=1)
    src = rows + shift
    valid = (src < limit) & (rows < n_used * tm)
    src_token = jnp.where(valid, sorted_tok[jnp.clip(src, 0, n_assign - 1)], 0).astype(i32)
    tile_rows = jnp.sum(valid.reshape(n_tiles, tm).astype(i32), axis=1)
    return (tile_expert, n_used.reshape(1).astype(i32), tile_rows, src_token,
            pos[:, 0].astype(i32), pos[:, 1].astype(i32))


def _final_norm_kernel(x_ref, w_ref, o_ref):
    x = x_ref[...]
    o_ref[...] = (x * lax.rsqrt(jnp.mean(x * x, axis=-1, keepdims=True) + EPS)) * w_ref[...]


def _final_norm(x, w):
    n, d = x.shape
    tm = min(TM_FFN, n)
    return pl.pallas_call(
        _final_norm_kernel,
        out_shape=jax.ShapeDtypeStruct((n, d), F32),
        grid=(n // tm,),
        in_specs=[pl.BlockSpec((tm, d), lambda i: (i, 0)), pl.BlockSpec((1, d), lambda i: (0, 0))],
        out_specs=pl.BlockSpec((tm, d), lambda i: (i, 0)),
        compiler_params=_params(1),
        name="final_norm",
    )(x, w)


def _retention_tables(ret_heads, seq):
    half = RET_HEAD_DIM // 2
    inv_freq = 1.0 / (ROPE_BASE ** (jnp.arange(half, dtype=F32) * 2.0 / RET_HEAD_DIM))
    ang = jnp.arange(seq, dtype=F32)[:, None] * inv_freq[None, :]
    log_gamma = jnp.log(1.0 - 2.0 ** (-5.0 - jnp.arange(ret_heads, dtype=F32)))
    j = jnp.arange(CHUNK, dtype=F32)
    diff = j[:, None] - j[None, :]
    dmask = jnp.where(diff >= 0, jnp.exp(log_gamma[:, None, None] * jnp.maximum(diff, 0.0)), 0.0)
    xi = jnp.exp(log_gamma[:, None] * (j[None, :] + 1.0))
    zeta = jnp.exp(log_gamma[:, None] * (CHUNK - 1.0 - j[None, :]))
    dec = jnp.exp(log_gamma * CHUNK)
    wide = (ret_heads, CHUNK, RET_HEAD_DIM)
    return (jnp.cos(ang), jnp.sin(ang), dmask,
            jnp.broadcast_to(xi[:, :, None], wide), jnp.broadcast_to(zeta[:, :, None], wide),
            jnp.broadcast_to(dec[:, None, None], (ret_heads, 1, RET_HEAD_DIM)))


def kernel(x, c, norm1_w, ada_w, ada_b, w_in, sg_w, sg_b, w_out, norm2_w, ffn_w_gate, ffn_w_up, ffn_w_down, router_w, router_b, moe_w_gate, moe_w_up, moe_w_down, final_norm_w):
    nb, seq, d = x.shape
    depth = w_in.shape[0]
    sg_heads = sg_w.shape[1]
    ret_heads = (w_in.shape[2] - 2 * sg_heads * SG_HEAD_DIM) // (4 * RET_HEAD_DIM)
    n = nb * seq
    assert seq % CHUNK == 0 and sg_w.shape[2] == CHUNK

    mod = _modulation(c, ada_w, ada_b).reshape(depth, nb, 6, d)
    cos, sin, dmask, xi, zeta, dec = _retention_tables(ret_heads, seq)
    xf = x.reshape(n, d)
    w_in_b, w_out_b = w_in.astype(BF16), w_out.astype(BF16)
    ffn_b = [w.astype(BF16) for w in (ffn_w_gate, ffn_w_up, ffn_w_down)]
    final_w = final_norm_w.reshape(1, d)
    for l in range(depth):
        z = _inproj(xf, norm1_w[l].reshape(1, d), mod[l], w_in_b, l, seq)
        sgb = jnp.broadcast_to(sg_b[l][:, :, None], (sg_heads, CHUNK, LANES))
        y = _mixer(z, cos, sin, sg_w[l], sgb, dmask, xi, zeta, dec, seq)
        i = l // 2
        nw2 = norm2_w[l].reshape(1, d)
        if l % 2 == 0:
            xf, h2 = _outproj(y, w_out_b, l, xf, mod[l], nw2, seq)
            xf = _ffn_dense(h2, *ffn_b, i, xf, mod[l], seq)
        else:
            n_experts = router_w.shape[2]
            tm = min(TM_FFN, n)
            xf, h2, route = _outproj(y, w_out_b, l, xf, mod[l], nw2, seq,
                                     router=(router_w[i], router_b[i]))
            tile_expert, n_used, tile_rows, src_token, p0, p1 = _routing_tables(route, n_experts, tm)
            o = _ffn_moe(tile_expert, n_used, tile_rows, src_token, h2, moe_w_gate, moe_w_up,
                         moe_w_down, i, tm)
            xf = _combine(o, p0, p1, route, xf, mod[l], seq,
                          final_w=final_w if l == depth - 1 else None)
    if depth % 2 == 1:
        xf = _final_norm(xf, final_w)
    return xf.reshape(nb, seq, d)
```

```python
import functools

import jax
import jax.numpy as jnp
from jax import lax
from jax.experimental import pallas as pl
from jax.experimental.pallas import tpu as pltpu

F32 = jnp.float32
BF16 = jnp.bfloat16

EPS = 1e-6
ROPE_BASE = 10000.0
SG_HEAD_DIM = 128
RET_HEAD_DIM = 256
CHUNK = 128
TOP_K = 2
LANES = 128

TM_INPROJ = 512
TC_MIXER = 512
MIXER_CHUNK_UNROLL = 2
TM_OUTPROJ = 512
ROW_CHUNKS_OUTPROJ = 2
TM_FFN = 1024
TF_FFN = 512
TN_DOWN_DENSE = 512
TN_DOWN_MOE = 512
ROW_BLOCK_MOE = 256
GATHER_ROWS_PER_STEP = 128
TR_COMBINE = 1024
TN_MOD = 512
DMA_ISSUE_UNROLL = 8
VMEM_LIMIT = 60 * 1024 * 1024


def _params(n_axes):
    return pltpu.CompilerParams(dimension_semantics=("arbitrary",) * n_axes,
                                vmem_limit_bytes=VMEM_LIMIT)


def _gelu_tanh(x):
    return 0.5 * x * (1.0 + jnp.tanh(0.7978845608028654 * (x + 0.044715 * (x * x * x))))


def _silu(x):
    return x * jax.nn.sigmoid(x)


def _pack_bf16_pairs(x):
    k = x.shape[1] // 2
    lo = lax.bitcast_convert_type(x[:, :k].astype(BF16).astype(F32), jnp.uint32)
    hi = lax.bitcast_convert_type(x[:, k:].astype(BF16).astype(F32), jnp.uint32)
    return (lo >> 16) | (hi & jnp.uint32(0xFFFF0000))


def _unpack_bf16_pairs(p):
    lo = lax.bitcast_convert_type(p << 16, F32)
    hi = lax.bitcast_convert_type(p & jnp.uint32(0xFFFF0000), F32)
    return jnp.concatenate([lo.astype(BF16), hi.astype(BF16)], axis=1)


def _rms_modulate(x, nw, shift, scale):
    y = x * lax.rsqrt(jnp.mean(x * x, axis=-1, keepdims=True) + EPS)
    return (y * nw) * (1.0 + scale) + shift


def _mod_kernel(cb_ref, w_ref, b_ref, o_ref):
    tn = w_ref.shape[1]
    for b in range(cb_ref.shape[0]):
        act = _silu(cb_ref[b])
        cols = [jnp.sum(w_ref[:, j * LANES:(j + 1) * LANES] * act, axis=0, keepdims=True)
                for j in range(tn // LANES)]
        o_ref[b:b + 1, :] = jnp.concatenate(cols, axis=1) + b_ref[...]


def _modulation(c, ada_w, ada_b):
    depth, d, n6 = ada_w.shape
    nb = c.shape[0]
    tn = min(TN_MOD, n6)
    cb = jnp.broadcast_to(c[:, :, None], (nb, d, LANES))
    return pl.pallas_call(
        _mod_kernel,
        out_shape=jax.ShapeDtypeStruct((depth, nb, n6), F32),
        grid=(depth, n6 // tn),
        in_specs=[pl.BlockSpec((nb, d, LANES), lambda l, j: (0, 0, 0)),
                  pl.BlockSpec((None, d, tn), lambda l, j: (l, 0, j)),
                  pl.BlockSpec((None, 1, tn), lambda l, j: (l, 0, j))],
        out_specs=pl.BlockSpec((None, nb, tn), lambda l, j: (l, 0, j)),
        compiler_params=_params(2),
        name="adaln_mod",
    )(cb, ada_w, ada_b.reshape(depth, 1, n6))


def _inproj_kernel(x_ref, nw_ref, mod_ref, w_ref, z_ref):
    h = _rms_modulate(x_ref[...], nw_ref[...], mod_ref[0:1, :], mod_ref[1:2, :])
    z_ref[...] = jnp.dot(h.astype(BF16), w_ref[...], preferred_element_type=F32).astype(z_ref.dtype)


def _inproj(x, nw, mod, w, layer, seq):
    n, d = x.shape
    nin = w.shape[2]
    tm = min(TM_INPROJ, seq)
    assert seq % tm == 0
    per_b = seq // tm
    return pl.pallas_call(
        _inproj_kernel,
        out_shape=jax.ShapeDtypeStruct((n, nin), BF16),
        grid=(n // tm,),
        in_specs=[pl.BlockSpec((tm, d), lambda i: (i, 0)),
                  pl.BlockSpec((1, d), lambda i: (0, 0)),
                  pl.BlockSpec((None, 6, d), lambda i: (i // per_b, 0, 0)),
                  pl.BlockSpec((None, d, nin), lambda i: (layer, 0, 0), pipeline_mode=pl.Buffered(1))],
        out_specs=pl.BlockSpec((tm, nin), lambda i: (i, 0)),
        compiler_params=_params(1),
        name="inproj",
    )(x, nw, mod, w)


def _mixer_kernel(z_ref, cos_ref, sin_ref, sgw_ref, sgb_ref, dmask_ref, xi_ref, zeta_ref,
                  dec_ref, y_ref, wm_ref, state_ref, *, sg_heads, ret_heads, steps_per_seq):
    sgw_w = sg_heads * SG_HEAD_DIM
    ret_w = ret_heads * RET_HEAD_DIM
    q_off, k_off = 2 * sgw_w, 2 * sgw_w + ret_w
    v_off, g_off = 2 * sgw_w + 2 * ret_w, 2 * sgw_w + 3 * ret_w
    half = RET_HEAD_DIM // 2
    step = pl.program_id(0)

    @pl.when(step == 0)
    def _():
        t = lax.broadcasted_iota(jnp.int32, (CHUNK, CHUNK), 0)
        s = lax.broadcasted_iota(jnp.int32, (CHUNK, CHUNK), 1)
        causal = (t >= s).astype(F32)
        for h in range(sg_heads):
            wm_ref[h] = (sgw_ref[h] * causal).astype(BF16)

    @pl.when(step % steps_per_seq == 0)
    def _():
        state_ref[...] = jnp.zeros_like(state_ref)

    def rotate(t, cos, sin):
        t1, t2 = t[:, :half], t[:, half:]
        return jnp.concatenate([t1 * cos - t2 * sin, t1 * sin + t2 * cos], axis=-1)

    def chunk(c, carry):
        rows = pl.ds(pl.multiple_of(c * CHUNK, CHUNK), CHUNK)
        cos = cos_ref[rows, :]
        sin = sin_ref[rows, :]
        for h in range(sg_heads):
            lo = h * SG_HEAD_DIM
            u = _gelu_tanh(z_ref[rows, lo:lo + SG_HEAD_DIM].astype(F32))
            v = _gelu_tanh(z_ref[rows, sgw_w + lo:sgw_w + lo + SG_HEAD_DIM].astype(F32))
            vc = v - jnp.mean(v, axis=-1, keepdims=True)
            vn = vc * lax.rsqrt(jnp.mean(vc * vc, axis=-1, keepdims=True) + EPS)
            sv = jnp.dot(wm_ref[h], vn.astype(BF16), preferred_element_type=F32) + sgb_ref[h]
            y_ref[rows, lo:lo + SG_HEAD_DIM] = (u * sv).astype(y_ref.dtype)
        for h in range(ret_heads):
            lo = h * RET_HEAD_DIM
            q = z_ref[rows, q_off + lo:q_off + lo + RET_HEAD_DIM].astype(F32)
            k = z_ref[rows, k_off + lo:k_off + lo + RET_HEAD_DIM].astype(F32)
            v = z_ref[rows, v_off + lo:v_off + lo + RET_HEAD_DIM]
            g = z_ref[rows, g_off + lo:g_off + lo + RET_HEAD_DIM].astype(F32)
            qr = rotate(q, cos, sin)
            kr = rotate(k, cos, sin) * (RET_HEAD_DIM ** -0.5)
            qb = qr.astype(BF16)
            scores = lax.dot_general(qb, kr.astype(BF16), (((1,), (1,)), ((), ())),
                                     preferred_element_type=F32) * dmask_ref[h]
            st = state_ref[h]
            yv = (jnp.dot(scores.astype(BF16), v, preferred_element_type=F32)
                  + jnp.dot(qb, st.astype(BF16), preferred_element_type=F32) * xi_ref[h])
            kz = (kr * zeta_ref[h]).astype(BF16)
            state_ref[h] = st * dec_ref[h] + lax.dot_general(
                kz, v, (((0,), (0,)), ((), ())), preferred_element_type=F32)
            yn = yv * lax.rsqrt(jnp.mean(yv * yv, axis=-1, keepdims=True) + EPS)
            y_ref[rows, sgw_w + lo:sgw_w + lo + RET_HEAD_DIM] = (_silu(g) * yn).astype(y_ref.dtype)
        return carry

    lax.fori_loop(0, z_ref.shape[0] // CHUNK, chunk, 0, unroll=MIXER_CHUNK_UNROLL)


def _mixer(z, cos, sin, sgw, sgb, dmask, xi, zeta, dec, seq):
    n, nin = z.shape
    sg_heads = sgw.shape[0]
    ret_heads = dmask.shape[0]
    mix = sg_heads * SG_HEAD_DIM + ret_heads * RET_HEAD_DIM
    tc = min(TC_MIXER, seq)
    per_seq = seq // tc
    const3 = lambda i: (0, 0, 0)
    kern = functools.partial(_mixer_kernel, sg_heads=sg_heads, ret_heads=ret_heads,
                             steps_per_seq=per_seq)
    return pl.pallas_call(
        kern,
        out_shape=jax.ShapeDtypeStruct((n, mix), BF16),
        grid=(n // tc,),
        in_specs=[pl.BlockSpec((tc, nin), lambda i: (i, 0)),
                  pl.BlockSpec((tc, LANES), lambda i: (i % per_seq, 0)),
                  pl.BlockSpec((tc, LANES), lambda i: (i % per_seq, 0)),
                  pl.BlockSpec(sgw.shape, const3),
                  pl.BlockSpec(sgb.shape, const3),
                  pl.BlockSpec(dmask.shape, const3),
                  pl.BlockSpec(xi.shape, const3),
                  pl.BlockSpec(zeta.shape, const3),
                  pl.BlockSpec(dec.shape, const3)],
        out_specs=pl.BlockSpec((tc, mix), lambda i: (i, 0)),
        scratch_shapes=[pltpu.VMEM((sg_heads, CHUNK, CHUNK), BF16),
                        pltpu.VMEM((ret_heads, RET_HEAD_DIM, RET_HEAD_DIM), F32)],
        compiler_params=_params(1),
        name="mixer",
    )(z, cos, sin, sgw, sgb, dmask, xi, zeta, dec)


def _route_top2(logits):
    lane = lax.broadcasted_iota(jnp.int32, logits.shape, 1)
    m1 = jnp.max(logits, axis=-1, keepdims=True)
    i1 = jnp.min(jnp.where(logits == m1, lane, LANES), axis=-1, keepdims=True)
    rest = jnp.where(lane == i1, -jnp.inf, logits)
    m2 = jnp.max(rest, axis=-1, keepdims=True)
    i2 = jnp.min(jnp.where(rest == m2, lane, LANES), axis=-1, keepdims=True)
    e = jnp.exp(m2 - m1)
    den = 1.0 + e
    w1 = 1.0 / den
    w2 = e / den
    return jnp.where(lane == 0, i1.astype(F32),
                     jnp.where(lane == 1, i2.astype(F32),
                               jnp.where(lane == 2, w1, jnp.where(lane == 3, w2, 0.0))))


def _router_logits(h2, rwt_ref, rb_ref, n_experts):
    lane = lax.broadcasted_iota(jnp.int32, (h2.shape[0], LANES), 1)
    logits = jnp.full((h2.shape[0], LANES), -jnp.inf, F32)
    for e in range(n_experts):
        s = jnp.sum(h2 * rwt_ref[e:e + 1, :], axis=-1, keepdims=True)
        logits = jnp.where(lane == e, s, logits)
    return logits + rb_ref[...]


def _outproj_kernel(*refs, n_experts, row_chunks):
    if n_experts:
        y_ref, w_ref, x_ref, mod_ref, nw_ref, rwt_ref, rb_ref, xo_ref, h2_ref, r_ref = refs
    else:
        y_ref, w_ref, x_ref, mod_ref, nw_ref, xo_ref, h2_ref = refs
    rc = y_ref.shape[0] // row_chunks
    for c in range(row_chunks):
        rows = slice(c * rc, (c + 1) * rc)
        yo = jnp.dot(y_ref[rows, :], w_ref[...], preferred_element_type=F32)
        xn = x_ref[rows, :] + mod_ref[2:3, :] * yo
        xo_ref[rows, :] = xn
        h2 = _rms_modulate(xn, nw_ref[...], mod_ref[3:4, :], mod_ref[4:5, :])
        if n_experts:
            h2_ref[rows] = _pack_bf16_pairs(h2).reshape(rc, -1, LANES)
            r_ref[rows, :] = _route_top2(_router_logits(h2, rwt_ref, rb_ref, n_experts))
        else:
            h2_ref[rows, :] = h2.astype(h2_ref.dtype)


def _outproj(y, w, layer, x, mod, nw, seq, router=None):
    n, d = x.shape
    mix = y.shape[1]
    tm = min(TM_OUTPROJ, seq)
    per_b = seq // tm
    row = lambda i: (i, 0)
    const = lambda i: (0, 0)
    in_specs = [pl.BlockSpec((tm, mix), row),
                pl.BlockSpec((None, mix, d), lambda i: (layer, 0, 0)),
                pl.BlockSpec((tm, d), row),
                pl.BlockSpec((None, 6, d), lambda i: (i // per_b, 0, 0)),
                pl.BlockSpec((1, d), const)]
    args = [y, w, x, mod, nw]
    out_shape = [jax.ShapeDtypeStruct((n, d), F32)]
    out_specs = [pl.BlockSpec((tm, d), row)]
    n_experts = 0
    if router is None:
        out_shape.append(jax.ShapeDtypeStruct((n, d), BF16))
        out_specs.append(pl.BlockSpec((tm, d), row))
    else:
        rw, rb = router
        n_experts = rw.shape[1]
        rb = jnp.pad(rb, (0, LANES - n_experts)).reshape(1, LANES)
        in_specs += [pl.BlockSpec((n_experts, d), const), pl.BlockSpec((1, LANES), const)]
        args += [rw.T, rb]
        slab = (d // 2 // LANES, LANES)
        out_shape += [jax.ShapeDtypeStruct((n, *slab), jnp.uint32),
                      jax.ShapeDtypeStruct((n, LANES), F32)]
        out_specs += [pl.BlockSpec((tm, *slab), lambda i: (i, 0, 0)),
                      pl.BlockSpec((tm, LANES), row)]
    return pl.pallas_call(
        functools.partial(_outproj_kernel, n_experts=n_experts, row_chunks=ROW_CHUNKS_OUTPROJ),
        out_shape=out_shape,
        grid=(n // tm,),
        in_specs=in_specs,
        out_specs=out_specs,
        compiler_params=_params(1),
        name="outproj_router" if n_experts else "outproj",
    )(*args)


def _ffn_up_step(h_ref, wg_ref, wu_ref, a_ref, j, tf, rows=None):
    rows = h_ref.shape[0] if rows is None else rows
    h = h_ref[:rows, :]
    g = jnp.dot(h, wg_ref[...].astype(BF16), preferred_element_type=F32)
    u = jnp.dot(h, wu_ref[...].astype(BF16), preferred_element_type=F32)
    a = (_silu(g) * u).astype(BF16)
    a_ref[:rows, pl.ds(pl.multiple_of(j * tf, tf), tf)] = a


def _ffn_down(a_ref, wd_ref, rows=None):
    rows = a_ref.shape[0] if rows is None else rows
    return jnp.dot(a_ref[:rows, :], wd_ref[...].astype(BF16), preferred_element_type=F32)


def _ffn_dense_kernel(h_ref, wg_ref, wu_ref, wd_ref, x_ref, mod_ref, o_ref, a_ref, *, nj, tf):
    j = pl.program_id(1)

    @pl.when(j < nj)
    def _():
        _ffn_up_step(h_ref, wg_ref, wu_ref, a_ref, j, tf)

    @pl.when(j >= nj)
    def _():
        o_ref[...] = x_ref[...] + mod_ref[5:6, :] * _ffn_down(a_ref, wd_ref)


def _ffn_dense(h2, wg, wu, wd, layer, x, mod, seq):
    n, d = x.shape
    f = wg.shape[2]
    tm = min(TM_FFN, seq)
    tf = min(TF_FFN, f)
    tn = min(TN_DOWN_DENSE, d)
    nj, nd = f // tf, d // tn
    per_b = seq // tm
    feat = lambda j: jnp.minimum(j, nj - 1)
    col = lambda j: jnp.maximum(j - nj, 0)
    return pl.pallas_call(
        functools.partial(_ffn_dense_kernel, nj=nj, tf=tf),
        out_shape=jax.ShapeDtypeStruct((n, d), F32),
        grid=(n // tm, nj + nd),
        in_specs=[pl.BlockSpec((tm, d), lambda i, j: (i, 0)),
                  pl.BlockSpec((None, d, tf), lambda i, j: (layer, 0, feat(j))),
                  pl.BlockSpec((None, d, tf), lambda i, j: (layer, 0, feat(j))),
                  pl.BlockSpec((None, f, tn), lambda i, j: (layer, 0, col(j))),
                  pl.BlockSpec((tm, tn), lambda i, j: (i, col(j))),
                  pl.BlockSpec((None, 6, tn), lambda i, j: (i // per_b, 0, col(j)))],
        out_specs=pl.BlockSpec((tm, tn), lambda i, j: (i, col(j))),
        scratch_shapes=[pltpu.VMEM((tm, f), BF16)],
        compiler_params=_params(2),
        name="ffn_dense",
    )(h2, wg, wu, wd, x, mod)


def _row_copy(src_hbm, dst_ref, src_row, dst_row, sem):
    return pltpu.make_async_copy(src_hbm.at[pl.ds(src_row, 1), :],
                                 dst_ref.at[pl.ds(dst_row, 1), :], sem)


def _ffn_moe_kernel(te_ref, nu_ref, nv_ref, src_ref, hp_hbm, wg_ref, wu_ref, wd_ref, o_ref,
                    a_ref, h_ref, stage_ref, sem, *, nj, tf):
    i = pl.program_id(0)
    j = pl.program_id(1)
    tm = h_ref.shape[0]
    n_used = nu_ref[0]
    active = i < n_used
    blocks = (nv_ref[i] + ROW_BLOCK_MOE - 1) // ROW_BLOCK_MOE

    def start_rows(tile, first, count):
        def body(k, carry):
            r = first + k
            pltpu.make_async_copy(hp_hbm.at[src_ref[tile * tm + r]], stage_ref.at[r], sem).start()
            return carry
        lax.fori_loop(0, count, body, 0, unroll=DMA_ISSUE_UNROLL)

    @pl.when(jnp.logical_and(i == 0, j == 0))
    def _():
        start_rows(0, 0, tm)

    @pl.when(jnp.logical_and(active, j == 0))
    def _():
        pltpu.make_async_copy(hp_hbm.at[pl.ds(0, tm)], stage_ref, sem).wait()
        h_ref[...] = _unpack_bf16_pairs(stage_ref[...].reshape(tm, -1))

    prefetching = jnp.logical_and(j >= 1, j <= tm // GATHER_ROWS_PER_STEP)

    @pl.when(jnp.logical_and(i + 1 < n_used, prefetching))
    def _():
        start_rows(i + 1, (j - 1) * GATHER_ROWS_PER_STEP, GATHER_ROWS_PER_STEP)

    for q in range(1, tm // ROW_BLOCK_MOE + 1):
        rows = q * ROW_BLOCK_MOE
        this = jnp.logical_and(active, blocks == q)

        @pl.when(jnp.logical_and(this, j < nj))
        def _():
            _ffn_up_step(h_ref, wg_ref, wu_ref, a_ref, j, tf, rows=rows)

        @pl.when(jnp.logical_and(this, j >= nj))
        def _():
            o_ref[:rows, :] = _ffn_down(a_ref, wd_ref, rows=rows)
            if rows < tm:
                o_ref[rows:, :] = jnp.zeros((tm - rows, o_ref.shape[1]), o_ref.dtype)

    @pl.when(jnp.logical_and(jnp.logical_not(active), j >= nj))
    def _():
        o_ref[...] = jnp.zeros_like(o_ref)


def _ffn_moe(tile_expert, n_used, tile_rows, src_token, hp, wg, wu, wd, layer, tm):
    r = src_token.shape[0]
    d, f = wg.shape[2:]
    tf = min(TF_FFN, f)
    tn = min(TN_DOWN_MOE, d)
    nj, nd = f // tf, d // tn
    assert tm % ROW_BLOCK_MOE == 0 and tm % GATHER_ROWS_PER_STEP == 0
    assert tm // GATHER_ROWS_PER_STEP <= nj + nd - 1

    def feat(i, j, nu):
        return jnp.where(i < nu[0], jnp.minimum(j, nj - 1), nj - 1)

    def col_in(i, j, nu):
        return jnp.where(i < nu[0], jnp.maximum(j - nj, 0), nd - 1)

    up_spec = pl.BlockSpec((None, None, d, tf),
                           lambda i, j, te, nu, nv, src: (layer, te[i], 0, feat(i, j, nu)))
    grid_spec = pltpu.PrefetchScalarGridSpec(
        num_scalar_prefetch=4,
        grid=(r // tm, nj + nd),
        in_specs=[pl.BlockSpec(memory_space=pl.ANY),
                  up_spec, up_spec,
                  pl.BlockSpec((None, None, f, tn),
                               lambda i, j, te, nu, nv, src: (layer, te[i], 0, col_in(i, j, nu)))],
        out_specs=pl.BlockSpec((tm, tn), lambda i, j, te, nu, nv, src: (i, jnp.maximum(j - nj, 0))),
        scratch_shapes=[pltpu.VMEM((tm, f), BF16), pltpu.VMEM((tm, d), BF16),
                        pltpu.VMEM((tm, *hp.shape[1:]), jnp.uint32), pltpu.SemaphoreType.DMA],
    )
    return pl.pallas_call(
        functools.partial(_ffn_moe_kernel, nj=nj, tf=tf),
        out_shape=jax.ShapeDtypeStruct((r, d), F32),
        grid_spec=grid_spec,
        compiler_params=_params(2),
        name="ffn_moe",
    )(tile_expert, n_used, tile_rows, src_token, hp, wg, wu, wd)


def _combine_kernel(*refs, final_norm):
    if final_norm:
        p0_ref, p1_ref, o_hbm, route_ref, x_ref, mod_ref, fw_ref, out_ref, buf0_ref, buf1_ref, sem = refs
    else:
        p0_ref, p1_ref, o_hbm, route_ref, x_ref, mod_ref, out_ref, buf0_ref, buf1_ref, sem = refs
    rows = buf0_ref.shape[0]

    def issue(k, carry):
        _row_copy(o_hbm, buf0_ref, p0_ref[0, k], k, sem).start()
        _row_copy(o_hbm, buf1_ref, p1_ref[0, k], k, sem).start()
        return carry

    lax.fori_loop(0, rows, issue, 0, unroll=DMA_ISSUE_UNROLL)
    pltpu.make_async_copy(o_hbm.at[pl.ds(0, rows), :], buf0_ref, sem).wait()
    pltpu.make_async_copy(o_hbm.at[pl.ds(0, rows), :], buf1_ref, sem).wait()
    w0 = route_ref[:, TOP_K:TOP_K + 1]
    w1 = route_ref[:, TOP_K + 1:TOP_K + 2]
    xn = x_ref[...] + mod_ref[5:6, :] * (w0 * buf0_ref[...] + w1 * buf1_ref[...])
    if final_norm:
        xn = (xn * lax.rsqrt(jnp.mean(xn * xn, axis=-1, keepdims=True) + EPS)) * fw_ref[...]
    out_ref[...] = xn


def _combine(o, p0, p1, route, x, mod, seq, final_w=None):
    n, d = x.shape
    tr = min(TR_COMBINE, seq)
    per_b = seq // tr
    idx_spec = pl.BlockSpec((None, 1, tr), lambda i: (i, 0, 0), memory_space=pltpu.SMEM)
    in_specs = [idx_spec, idx_spec,
                pl.BlockSpec(memory_space=pl.ANY),
                pl.BlockSpec((tr, LANES), lambda i: (i, 0)),
                pl.BlockSpec((tr, d), lambda i: (i, 0)),
                pl.BlockSpec((None, 6, d), lambda i: (i // per_b, 0, 0))]
    args = [p0.reshape(n // tr, 1, tr), p1.reshape(n // tr, 1, tr), o, route, x, mod]
    if final_w is not None:
        in_specs.append(pl.BlockSpec((1, d), lambda i: (0, 0)))
        args.append(final_w)
    return pl.pallas_call(
        functools.partial(_combine_kernel, final_norm=final_w is not None),
        out_shape=jax.ShapeDtypeStruct((n, d), F32),
        grid=(n // tr,),
        in_specs=in_specs,
        out_specs=pl.BlockSpec((tr, d), lambda i: (i, 0)),
        scratch_shapes=[pltpu.VMEM((tr, d), F32), pltpu.VMEM((tr, d), F32),
                        pltpu.SemaphoreType.DMA],
        compiler_params=_params(1),
        name="moe_combine",
    )(*args)


def _routing_tables(route, n_experts, tm):
    n = route.shape[0]
    i32 = jnp.int32
    ids = route[:, :TOP_K].astype(i32).reshape(-1)
    n_assign = n * TOP_K
    n_tiles = n_assign // tm + n_experts
    experts = jnp.arange(n_experts, dtype=i32)
    onehot = (ids[:, None] == experts[None, :]).astype(i32)
    ranks = jnp.cumsum(onehot, axis=0)
    counts = ranks[-1]
    offs = jnp.cumsum(counts) - counts
    tiles_per = (counts + tm - 1) // tm
    tile_end = jnp.cumsum(tiles_per)
    n_used = tile_end[-1]
    pad_offs = (tile_end - tiles_per) * tm
    pos = jnp.sum(onehot * (pad_offs[None, :] + ranks - 1), axis=1).reshape(n, TOP_K)
    tile_ids = jnp.arange(n_tiles, dtype=i32)
    tile_expert = jnp.sum((tile_ids[:, None] >= tile_end[None, :]).astype(i32), axis=1)
    last_expert = jnp.sum((n_used - 1 >= tile_end).astype(i32))
    tile_expert = jnp.where(tile_ids < n_used, tile_expert, last_expert).astype(i32)
    _, sorted_tok = lax.sort((ids, jnp.arange(n_assign, dtype=i32) // TOP_K), num_keys=1,
                             is_stable=True)
    row_expert = jnp.repeat(tile_expert, tm)
    rows = jnp.arange(n_tiles * tm, dtype=i32)
    shift = jnp.sum((row_expert[:, None] == experts[None, :]) * (offs - pad_offs)[None, :], axis=1)
    limit = jnp.sum((row_expert[:, None] == experts[None, :]) * (offs + counts)[None, :], axis=1)
    src = rows + shift
    valid = (src < limit) & (rows < n_used * tm)
    src_token = jnp.where(valid, sorted_tok[jnp.clip(src, 0, n_assign - 1)], 0).astype(i32)
    tile_rows = jnp.sum(valid.reshape(n_tiles, tm).astype(i32), axis=1)
    return (tile_expert, n_used.reshape(1).astype(i32), tile_rows, src_token,
            pos[:, 0].astype(i32), pos[:, 1].astype(i32))


def _final_norm_kernel(x_ref, w_ref, o_ref):
    x = x_ref[...]
    o_ref[...] = (x * lax.rsqrt(jnp.mean(x * x, axis=-1, keepdims=True) + EPS)) * w_ref[...]


def _final_norm(x, w):
    n, d = x.shape
    tm = min(TM_FFN, n)
    return pl.pallas_call(
        _final_norm_kernel,
        out_shape=jax.ShapeDtypeStruct((n, d), F32),
        grid=(n // tm,),
        in_specs=[pl.BlockSpec((tm, d), lambda i: (i, 0)), pl.BlockSpec((1, d), lambda i: (0, 0))],
        out_specs=pl.BlockSpec((tm, d), lambda i: (i, 0)),
        compiler_params=_params(1),
        name="final_norm",
    )(x, w)


def _retention_tables(ret_heads, seq):
    half = RET_HEAD_DIM // 2
    inv_freq = 1.0 / (ROPE_BASE ** (jnp.arange(half, dtype=F32) * 2.0 / RET_HEAD_DIM))
    ang = jnp.arange(seq, dtype=F32)[:, None] * inv_freq[None, :]
    log_gamma = jnp.log(1.0 - 2.0 ** (-5.0 - jnp.arange(ret_heads, dtype=F32)))
    j = jnp.arange(CHUNK, dtype=F32)
    diff = j[:, None] - j[None, :]
    dmask = jnp.where(diff >= 0, jnp.exp(log_gamma[:, None, None] * jnp.maximum(diff, 0.0)), 0.0)
    xi = jnp.exp(log_gamma[:, None] * (j[None, :] + 1.0))
    zeta = jnp.exp(log_gamma[:, None] * (CHUNK - 1.0 - j[None, :]))
    dec = jnp.exp(log_gamma * CHUNK)
    wide = (ret_heads, CHUNK, RET_HEAD_DIM)
    return (jnp.cos(ang), jnp.sin(ang), dmask,
            jnp.broadcast_to(xi[:, :, None], wide), jnp.broadcast_to(zeta[:, :, None], wide),
            jnp.broadcast_to(dec[:, None, None], (ret_heads, 1, RET_HEAD_DIM)))


def kernel(x, c, norm1_w, ada_w, ada_b, w_in, sg_w, sg_b, w_out, norm2_w, ffn_w_gate, ffn_w_up, ffn_w_down, router_w, router_b, moe_w_gate, moe_w_up, moe_w_down, final_norm_w):
    nb, seq, d = x.shape
    depth = w_in.shape[0]
    sg_heads = sg_w.shape[1]
    ret_heads = (w_in.shape[2] - 2 * sg_heads * SG_HEAD_DIM) // (4 * RET_HEAD_DIM)
    n = nb * seq
    assert seq % CHUNK == 0 and sg_w.shape[2] == CHUNK

    mod = _modulation(c, ada_w, ada_b).reshape(depth, nb, 6, d)
    cos, sin, dmask, xi, zeta, dec = _retention_tables(ret_heads, seq)
    xf = x.reshape(n, d)
    w_in_b, w_out_b = w_in.astype(BF16), w_out.astype(BF16)
    ffn_b = [w.astype(BF16) for w in (ffn_w_gate, ffn_w_up, ffn_w_down)]
    moe_wd_b = moe_w_down.astype(BF16)
    final_w = final_norm_w.reshape(1, d)
    for l in range(depth):
        z = _inproj(xf, norm1_w[l].reshape(1, d), mod[l], w_in_b, l, seq)
        sgb = jnp.broadcast_to(sg_b[l][:, :, None], (sg_heads, CHUNK, LANES))
        y = _mixer(z, cos, sin, sg_w[l], sgb, dmask, xi, zeta, dec, seq)
        i = l // 2
        nw2 = norm2_w[l].reshape(1, d)
        if l % 2 == 0:
            xf, h2 = _outproj(y, w_out_b, l, xf, mod[l], nw2, seq)
            xf = _ffn_dense(h2, *ffn_b, i, xf, mod[l], seq)
        else:
            n_experts = router_w.shape[2]
            tm = min(TM_FFN, n)
            xf, h2, route = _outproj(y, w_out_b, l, xf, mod[l], nw2, seq,
                                     router=(router_w[i], router_b[i]))
            tile_expert, n_used, tile_rows, src_token, p0, p1 = _routing_tables(route, n_experts, tm)
            o = _ffn_moe(tile_expert, n_used, tile_rows, src_token, h2, moe_w_gate, moe_w_up,
                         moe_wd_b, i, tm)
            xf = _combine(o, p0, p1, route, xf, mod[l], seq,
                          final_w=final_w if l == depth - 1 else None)
    if depth % 2 == 1:
        xf = _final_norm(xf, final_w)
    return xf.reshape(nb, seq, d)
```

```python
import functools

import jax
import jax.numpy as jnp
from jax import lax
from jax.experimental import pallas as pl
from jax.experimental.pallas import tpu as pltpu

F32 = jnp.float32
BF16 = jnp.bfloat16

EPS = 1e-6
ROPE_BASE = 10000.0
SG_HEAD_DIM = 128
RET_HEAD_DIM = 256
CHUNK = 128
TOP_K = 2
LANES = 128

TM_INPROJ = 512
TC_MIXER = 512
MIXER_CHUNK_UNROLL = 2
TM_OUTPROJ = 512
ROW_CHUNKS_OUTPROJ = 2
TM_FFN = 1024
TF_FFN = 512
TN_DOWN_DENSE = 512
TN_DOWN_MOE = 512
ROW_BLOCK_MOE = 256
GATHER_ROWS_PER_STEP = 128
TR_COMBINE = 512
TN_MOD = 512
DMA_ISSUE_UNROLL = 8
VMEM_LIMIT = 60 * 1024 * 1024


def _params(n_axes):
    return pltpu.CompilerParams(dimension_semantics=("arbitrary",) * n_axes,
                                vmem_limit_bytes=VMEM_LIMIT)


def _gelu_tanh(x):
    return 0.5 * x * (1.0 + jnp.tanh(0.7978845608028654 * (x + 0.044715 * (x * x * x))))


def _silu(x):
    return x * jax.nn.sigmoid(x)


def _pack_bf16_pairs(x):
    k = x.shape[1] // 2
    lo = lax.bitcast_convert_type(x[:, :k].astype(BF16).astype(F32), jnp.uint32)
    hi = lax.bitcast_convert_type(x[:, k:].astype(BF16).astype(F32), jnp.uint32)
    return (lo >> 16) | (hi & jnp.uint32(0xFFFF0000))


def _unpack_bf16_pairs(p):
    lo = lax.bitcast_convert_type(p << 16, F32)
    hi = lax.bitcast_convert_type(p & jnp.uint32(0xFFFF0000), F32)
    return jnp.concatenate([lo.astype(BF16), hi.astype(BF16)], axis=1)


def _rms_modulate(x, nw, shift, scale):
    y = x * lax.rsqrt(jnp.mean(x * x, axis=-1, keepdims=True) + EPS)
    return (y * nw) * (1.0 + scale) + shift


def _mod_kernel(cb_ref, w_ref, b_ref, o_ref):
    tn = w_ref.shape[1]
    for b in range(cb_ref.shape[0]):
        act = _silu(cb_ref[b])
        cols = [jnp.sum(w_ref[:, j * LANES:(j + 1) * LANES] * act, axis=0, keepdims=True)
                for j in range(tn // LANES)]
        o_ref[b:b + 1, :] = jnp.concatenate(cols, axis=1) + b_ref[...]


def _modulation(c, ada_w, ada_b):
    depth, d, n6 = ada_w.shape
    nb = c.shape[0]
    tn = min(TN_MOD, n6)
    cb = jnp.broadcast_to(c[:, :, None], (nb, d, LANES))
    return pl.pallas_call(
        _mod_kernel,
        out_shape=jax.ShapeDtypeStruct((depth, nb, n6), F32),
        grid=(depth, n6 // tn),
        in_specs=[pl.BlockSpec((nb, d, LANES), lambda l, j: (0, 0, 0)),
                  pl.BlockSpec((None, d, tn), lambda l, j: (l, 0, j)),
                  pl.BlockSpec((None, 1, tn), lambda l, j: (l, 0, j))],
        out_specs=pl.BlockSpec((None, nb, tn), lambda l, j: (l, 0, j)),
        compiler_params=_params(2),
        name="adaln_mod",
    )(cb, ada_w, ada_b.reshape(depth, 1, n6))


def _inproj_kernel(x_ref, nw_ref, mod_ref, w_ref, z_ref):
    h = _rms_modulate(x_ref[...], nw_ref[...], mod_ref[0:1, :], mod_ref[1:2, :])
    z_ref[...] = jnp.dot(h.astype(BF16), w_ref[...], preferred_element_type=F32).astype(z_ref.dtype)


def _inproj(x, nw, mod, w, layer, seq):
    n, d = x.shape
    nin = w.shape[2]
    tm = min(TM_INPROJ, seq)
    assert seq % tm == 0
    per_b = seq // tm
    return pl.pallas_call(
        _inproj_kernel,
        out_shape=jax.ShapeDtypeStruct((n, nin), BF16),
        grid=(n // tm,),
        in_specs=[pl.BlockSpec((tm, d), lambda i: (i, 0)),
                  pl.BlockSpec((1, d), lambda i: (0, 0)),
                  pl.BlockSpec((None, 6, d), lambda i: (i // per_b, 0, 0)),
                  pl.BlockSpec((None, d, nin), lambda i: (layer, 0, 0), pipeline_mode=pl.Buffered(1))],
        out_specs=pl.BlockSpec((tm, nin), lambda i: (i, 0)),
        compiler_params=_params(1),
        name="inproj",
    )(x, nw, mod, w)


def _mixer_kernel(z_ref, cos_ref, sin_ref, sgw_ref, sgb_ref, dmask_ref, xi_ref, zeta_ref,
                  dec_ref, y_ref, wm_ref, state_ref, *, sg_heads, ret_heads, steps_per_seq):
    sgw_w = sg_heads * SG_HEAD_DIM
    ret_w = ret_heads * RET_HEAD_DIM
    q_off, k_off = 2 * sgw_w, 2 * sgw_w + ret_w
    v_off, g_off = 2 * sgw_w + 2 * ret_w, 2 * sgw_w + 3 * ret_w
    half = RET_HEAD_DIM // 2
    step = pl.program_id(0)

    @pl.when(step == 0)
    def _():
        t = lax.broadcasted_iota(jnp.int32, (CHUNK, CHUNK), 0)
        s = lax.broadcasted_iota(jnp.int32, (CHUNK, CHUNK), 1)
        causal = (t >= s).astype(F32)
        for h in range(sg_heads):
            wm_ref[h] = (sgw_ref[h] * causal).astype(BF16)

    @pl.when(step % steps_per_seq == 0)
    def _():
        state_ref[...] = jnp.zeros_like(state_ref)

    def rotate(t, cos, sin):
        t1, t2 = t[:, :half], t[:, half:]
        return jnp.concatenate([t1 * cos - t2 * sin, t1 * sin + t2 * cos], axis=-1)

    def chunk(c, carry):
        rows = pl.ds(pl.multiple_of(c * CHUNK, CHUNK), CHUNK)
        cos = cos_ref[rows, :]
        sin = sin_ref[rows, :]
        for h in range(sg_heads):
            lo = h * SG_HEAD_DIM
            u = _gelu_tanh(z_ref[rows, lo:lo + SG_HEAD_DIM].astype(F32))
            v = _gelu_tanh(z_ref[rows, sgw_w + lo:sgw_w + lo + SG_HEAD_DIM].astype(F32))
            vc = v - jnp.mean(v, axis=-1, keepdims=True)
            vn = vc * lax.rsqrt(jnp.mean(vc * vc, axis=-1, keepdims=True) + EPS)
            sv = jnp.dot(wm_ref[h], vn.astype(BF16), preferred_element_type=F32) + sgb_ref[h]
            y_ref[rows, lo:lo + SG_HEAD_DIM] = (u * sv).astype(y_ref.dtype)
        for h in range(ret_heads):
            lo = h * RET_HEAD_DIM
            q = z_ref[rows, q_off + lo:q_off + lo + RET_HEAD_DIM].astype(F32)
            k = z_ref[rows, k_off + lo:k_off + lo + RET_HEAD_DIM].astype(F32)
            v = z_ref[rows, v_off + lo:v_off + lo + RET_HEAD_DIM]
            g = z_ref[rows, g_off + lo:g_off + lo + RET_HEAD_DIM].astype(F32)
            qr = rotate(q, cos, sin)
            kr = rotate(k, cos, sin) * (RET_HEAD_DIM ** -0.5)
            qb = qr.astype(BF16)
            scores = lax.dot_general(qb, kr.astype(BF16), (((1,), (1,)), ((), ())),
                                     preferred_element_type=F32) * dmask_ref[h]
            st = state_ref[h]
            yv = (jnp.dot(scores.astype(BF16), v, preferred_element_type=F32)
                  + jnp.dot(qb, st.astype(BF16), preferred_element_type=F32) * xi_ref[h])
            kz = (kr * zeta_ref[h]).astype(BF16)
            state_ref[h] = st * dec_ref[h] + lax.dot_general(
                kz, v, (((0,), (0,)), ((), ())), preferred_element_type=F32)
            yn = yv * lax.rsqrt(jnp.mean(yv * yv, axis=-1, keepdims=True) + EPS)
            y_ref[rows, sgw_w + lo:sgw_w + lo + RET_HEAD_DIM] = (_silu(g) * yn).astype(y_ref.dtype)
        return carry

    lax.fori_loop(0, z_ref.shape[0] // CHUNK, chunk, 0, unroll=MIXER_CHUNK_UNROLL)


def _mixer(z, cos, sin, sgw, sgb, dmask, xi, zeta, dec, seq):
    n, nin = z.shape
    sg_heads = sgw.shape[0]
    ret_heads = dmask.shape[0]
    mix = sg_heads * SG_HEAD_DIM + ret_heads * RET_HEAD_DIM
    tc = min(TC_MIXER, seq)
    per_seq = seq // tc
    const3 = lambda i: (0, 0, 0)
    kern = functools.partial(_mixer_kernel, sg_heads=sg_heads, ret_heads=ret_heads,
                             steps_per_seq=per_seq)
    return pl.pallas_call(
        kern,
        out_shape=jax.ShapeDtypeStruct((n, mix), BF16),
        grid=(n // tc,),
        in_specs=[pl.BlockSpec((tc, nin), lambda i: (i, 0)),
                  pl.BlockSpec((tc, LANES), lambda i: (i % per_seq, 0)),
                  pl.BlockSpec((tc, LANES), lambda i: (i % per_seq, 0)),
                  pl.BlockSpec(sgw.shape, const3),
                  pl.BlockSpec(sgb.shape, const3),
                  pl.BlockSpec(dmask.shape, const3),
                  pl.BlockSpec(xi.shape, const3),
                  pl.BlockSpec(zeta.shape, const3),
                  pl.BlockSpec(dec.shape, const3)],
        out_specs=pl.BlockSpec((tc, mix), lambda i: (i, 0)),
        scratch_shapes=[pltpu.VMEM((sg_heads, CHUNK, CHUNK), BF16),
                        pltpu.VMEM((ret_heads, RET_HEAD_DIM, RET_HEAD_DIM), F32)],
        compiler_params=_params(1),
        name="mixer",
    )(z, cos, sin, sgw, sgb, dmask, xi, zeta, dec)


def _route_top2(logits):
    lane = lax.broadcasted_iota(jnp.int32, logits.shape, 1)
    m1 = jnp.max(logits, axis=-1, keepdims=True)
    i1 = jnp.min(jnp.where(logits == m1, lane, LANES), axis=-1, keepdims=True)
    rest = jnp.where(lane == i1, -jnp.inf, logits)
    m2 = jnp.max(rest, axis=-1, keepdims=True)
    i2 = jnp.min(jnp.where(rest == m2, lane, LANES), axis=-1, keepdims=True)
    e = jnp.exp(m2 - m1)
    den = 1.0 + e
    w1 = 1.0 / den
    w2 = e / den
    return jnp.where(lane == 0, i1.astype(F32),
                     jnp.where(lane == 1, i2.astype(F32),
                               jnp.where(lane == 2, w1, jnp.where(lane == 3, w2, 0.0))))


def _router_logits(h2, rwt_ref, rb_ref, n_experts):
    lane = lax.broadcasted_iota(jnp.int32, (h2.shape[0], LANES), 1)
    logits = jnp.full((h2.shape[0], LANES), -jnp.inf, F32)
    for e in range(n_experts):
        s = jnp.sum(h2 * rwt_ref[e:e + 1, :], axis=-1, keepdims=True)
        logits = jnp.where(lane == e, s, logits)
    return logits + rb_ref[...]


def _outproj_kernel(*refs, n_experts, row_chunks):
    if n_experts:
        y_ref, w_ref, x_ref, mod_ref, nw_ref, rwt_ref, rb_ref, xo_ref, h2_ref, r_ref = refs
    else:
        y_ref, w_ref, x_ref, mod_ref, nw_ref, xo_ref, h2_ref = refs
    rc = y_ref.shape[0] // row_chunks
    for c in range(row_chunks):
        rows = slice(c * rc, (c + 1) * rc)
        yo = jnp.dot(y_ref[rows, :], w_ref[...], preferred_element_type=F32)
        xn = x_ref[rows, :] + mod_ref[2:3, :] * yo
        xo_ref[rows, :] = xn
        h2 = _rms_modulate(xn, nw_ref[...], mod_ref[3:4, :], mod_ref[4:5, :])
        if n_experts:
            h2_ref[rows] = _pack_bf16_pairs(h2).reshape(rc, -1, LANES)
            r_ref[rows, :] = _route_top2(_router_logits(h2, rwt_ref, rb_ref, n_experts))
        else:
            h2_ref[rows, :] = h2.astype(h2_ref.dtype)


def _outproj(y, w, layer, x, mod, nw, seq, router=None):
    n, d = x.shape
    mix = y.shape[1]
    tm = min(TM_OUTPROJ, seq)
    per_b = seq // tm
    row = lambda i: (i, 0)
    const = lambda i: (0, 0)
    in_specs = [pl.BlockSpec((tm, mix), row),
                pl.BlockSpec((None, mix, d), lambda i: (layer, 0, 0)),
                pl.BlockSpec((tm, d), row),
                pl.BlockSpec((None, 6, d), lambda i: (i // per_b, 0, 0)),
                pl.BlockSpec((1, d), const)]
    args = [y, w, x, mod, nw]
    out_shape = [jax.ShapeDtypeStruct((n, d), F32)]
    out_specs = [pl.BlockSpec((tm, d), row)]
    n_experts = 0
    if router is None:
        out_shape.append(jax.ShapeDtypeStruct((n, d), BF16))
        out_specs.append(pl.BlockSpec((tm, d), row))
    else:
        rw, rb = router
        n_experts = rw.shape[1]
        rb = jnp.pad(rb, (0, LANES - n_experts)).reshape(1, LANES)
        in_specs += [pl.BlockSpec((n_experts, d), const), pl.BlockSpec((1, LANES), const)]
        args += [rw.T, rb]
        slab = (d // 2 // LANES, LANES)
        out_shape += [jax.ShapeDtypeStruct((n, *slab), jnp.uint32),
                      jax.ShapeDtypeStruct((n, LANES), F32)]
        out_specs += [pl.BlockSpec((tm, *slab), lambda i: (i, 0, 0)),
                      pl.BlockSpec((tm, LANES), row)]
    return pl.pallas_call(
        functools.partial(_outproj_kernel, n_experts=n_experts, row_chunks=ROW_CHUNKS_OUTPROJ),
        out_shape=out_shape,
        grid=(n // tm,),
        in_specs=in_specs,
        out_specs=out_specs,
        compiler_params=_params(1),
        name="outproj_router" if n_experts else "outproj",
    )(*args)


def _ffn_up_step(h_ref, wg_ref, wu_ref, a_ref, j, tf, rows=None):
    rows = h_ref.shape[0] if rows is None else rows
    h = h_ref[:rows, :]
    g = jnp.dot(h, wg_ref[...].astype(BF16), preferred_element_type=F32)
    u = jnp.dot(h, wu_ref[...].astype(BF16), preferred_element_type=F32)
    a = (_silu(g) * u).astype(BF16)
    a_ref[:rows, pl.ds(pl.multiple_of(j * tf, tf), tf)] = a


def _ffn_down(a_ref, wd_ref, rows=None):
    rows = a_ref.shape[0] if rows is None else rows
    return jnp.dot(a_ref[:rows, :], wd_ref[...].astype(BF16), preferred_element_type=F32)


def _ffn_dense_kernel(h_ref, wg_ref, wu_ref, wd_ref, x_ref, mod_ref, o_ref, a_ref, *, nj, tf):
    j = pl.program_id(1)

    @pl.when(j < nj)
    def _():
        _ffn_up_step(h_ref, wg_ref, wu_ref, a_ref, j, tf)

    @pl.when(j >= nj)
    def _():
        o_ref[...] = x_ref[...] + mod_ref[5:6, :] * _ffn_down(a_ref, wd_ref)


def _ffn_dense(h2, wg, wu, wd, layer, x, mod, seq):
    n, d = x.shape
    f = wg.shape[2]
    tm = min(TM_FFN, seq)
    tf = min(TF_FFN, f)
    tn = min(TN_DOWN_DENSE, d)
    nj, nd = f // tf, d // tn
    per_b = seq // tm
    feat = lambda j: jnp.minimum(j, nj - 1)
    col = lambda j: jnp.maximum(j - nj, 0)
    return pl.pallas_call(
        functools.partial(_ffn_dense_kernel, nj=nj, tf=tf),
        out_shape=jax.ShapeDtypeStruct((n, d), F32),
        grid=(n // tm, nj + nd),
        in_specs=[pl.BlockSpec((tm, d), lambda i, j: (i, 0)),
                  pl.BlockSpec((None, d, tf), lambda i, j: (layer, 0, feat(j))),
                  pl.BlockSpec((None, d, tf), lambda i, j: (layer, 0, feat(j))),
                  pl.BlockSpec((None, f, tn), lambda i, j: (layer, 0, col(j))),
                  pl.BlockSpec((tm, tn), lambda i, j: (i, col(j))),
                  pl.BlockSpec((None, 6, tn), lambda i, j: (i // per_b, 0, col(j)))],
        out_specs=pl.BlockSpec((tm, tn), lambda i, j: (i, col(j))),
        scratch_shapes=[pltpu.VMEM((tm, f), BF16)],
        compiler_params=_params(2),
        name="ffn_dense",
    )(h2, wg, wu, wd, x, mod)


def _row_copy(src_hbm, dst_ref, src_row, dst_row, sem):
    return pltpu.make_async_copy(src_hbm.at[pl.ds(src_row, 1), :],
                                 dst_ref.at[pl.ds(dst_row, 1), :], sem)


def _ffn_moe_kernel(te_ref, nu_ref, nv_ref, src_ref, hp_hbm, wg_ref, wu_ref, wd_ref, o_ref,
                    a_ref, h_ref, stage_ref, sem, *, nj, tf):
    i = pl.program_id(0)
    j = pl.program_id(1)
    tm = h_ref.shape[0]
    n_used = nu_ref[0]
    active = i < n_used
    blocks = (nv_ref[i] + ROW_BLOCK_MOE - 1) // ROW_BLOCK_MOE

    def start_rows(tile, first, count):
        def body(k, carry):
            r = first + k
            pltpu.make_async_copy(hp_hbm.at[src_ref[tile * tm + r]], stage_ref.at[r], sem).start()
            return carry
        lax.fori_loop(0, count, body, 0, unroll=DMA_ISSUE_UNROLL)

    @pl.when(jnp.logical_and(i == 0, j == 0))
    def _():
        start_rows(0, 0, tm)

    @pl.when(jnp.logical_and(active, j == 0))
    def _():
        pltpu.make_async_copy(hp_hbm.at[pl.ds(0, tm)], stage_ref, sem).wait()
        h_ref[...] = _unpack_bf16_pairs(stage_ref[...].reshape(tm, -1))

    prefetching = jnp.logical_and(j >= 1, j <= tm // GATHER_ROWS_PER_STEP)

    @pl.when(jnp.logical_and(i + 1 < n_used, prefetching))
    def _():
        start_rows(i + 1, (j - 1) * GATHER_ROWS_PER_STEP, GATHER_ROWS_PER_STEP)

    for q in range(1, tm // ROW_BLOCK_MOE + 1):
        rows = q * ROW_BLOCK_MOE
        this = jnp.logical_and(active, blocks == q)

        @pl.when(jnp.logical_and(this, j < nj))
        def _():
            _ffn_up_step(h_ref, wg_ref, wu_ref, a_ref, j, tf, rows=rows)

        @pl.when(jnp.logical_and(this, j >= nj))
        def _():
            o_ref[:rows, :] = _ffn_down(a_ref, wd_ref, rows=rows)
            if rows < tm:
                o_ref[rows:, :] = jnp.zeros((tm - rows, o_ref.shape[1]), o_ref.dtype)

    @pl.when(jnp.logical_and(jnp.logical_not(active), j >= nj))
    def _():
        o_ref[...] = jnp.zeros_like(o_ref)


def _ffn_moe(tile_expert, n_used, tile_rows, src_token, hp, wg, wu, wd, layer, tm):
    r = src_token.shape[0]
    d, f = wg.shape[2:]
    tf = min(TF_FFN, f)
    tn = min(TN_DOWN_MOE, d)
    nj, nd = f // tf, d // tn
    assert tm % ROW_BLOCK_MOE == 0 and tm % GATHER_ROWS_PER_STEP == 0
    assert tm // GATHER_ROWS_PER_STEP <= nj + nd - 1

    def feat(i, j, nu):
        return jnp.where(i < nu[0], jnp.minimum(j, nj - 1), nj - 1)

    def col_in(i, j, nu):
        return jnp.where(i < nu[0], jnp.maximum(j - nj, 0), nd - 1)

    up_spec = pl.BlockSpec((None, None, d, tf),
                           lambda i, j, te, nu, nv, src: (layer, te[i], 0, feat(i, j, nu)))
    grid_spec = pltpu.PrefetchScalarGridSpec(
        num_scalar_prefetch=4,
        grid=(r // tm, nj + nd),
        in_specs=[pl.BlockSpec(memory_space=pl.ANY),
                  up_spec, up_spec,
                  pl.BlockSpec((None, None, f, tn),
                               lambda i, j, te, nu, nv, src: (layer, te[i], 0, col_in(i, j, nu)))],
        out_specs=pl.BlockSpec((tm, tn), lambda i, j, te, nu, nv, src: (i, jnp.maximum(j - nj, 0))),
        scratch_shapes=[pltpu.VMEM((tm, f), BF16), pltpu.VMEM((tm, d), BF16),
                        pltpu.VMEM((tm, *hp.shape[1:]), jnp.uint32), pltpu.SemaphoreType.DMA],
    )
    return pl.pallas_call(
        functools.partial(_ffn_moe_kernel, nj=nj, tf=tf),
        out_shape=jax.ShapeDtypeStruct((r, d), F32),
        grid_spec=grid_spec,
        compiler_params=_params(2),
        name="ffn_moe",
    )(tile_expert, n_used, tile_rows, src_token, hp, wg, wu, wd)


def _combine_kernel(*refs, final_norm):
    if final_norm:
        p0_ref, p1_ref, o_hbm, route_ref, x_ref, mod_ref, fw_ref, out_ref, buf0_ref, buf1_ref, sems = refs
    else:
        p0_ref, p1_ref, o_hbm, route_ref, x_ref, mod_ref, out_ref, buf0_ref, buf1_ref, sems = refs
    rows = buf0_ref.shape[1]
    step = pl.program_id(0)
    slot = step % 2

    def start_rows(of_step, into):
        def issue(k, carry):
            t = of_step * rows + k
            _row_copy(o_hbm, buf0_ref.at[into], p0_ref[t], k, sems.at[into]).start()
            _row_copy(o_hbm, buf1_ref.at[into], p1_ref[t], k, sems.at[into]).start()
            return carry
        lax.fori_loop(0, rows, issue, 0, unroll=DMA_ISSUE_UNROLL)

    @pl.when(step == 0)
    def _():
        start_rows(0, 0)

    @pl.when(step + 1 < pl.num_programs(0))
    def _():
        start_rows(step + 1, 1 - slot)

    pltpu.make_async_copy(o_hbm.at[pl.ds(0, rows), :], buf0_ref.at[slot], sems.at[slot]).wait()
    pltpu.make_async_copy(o_hbm.at[pl.ds(0, rows), :], buf1_ref.at[slot], sems.at[slot]).wait()
    w0 = route_ref[:, TOP_K:TOP_K + 1]
    w1 = route_ref[:, TOP_K + 1:TOP_K + 2]
    xn = x_ref[...] + mod_ref[5:6, :] * (w0 * buf0_ref[slot] + w1 * buf1_ref[slot])
    if final_norm:
        xn = (xn * lax.rsqrt(jnp.mean(xn * xn, axis=-1, keepdims=True) + EPS)) * fw_ref[...]
    out_ref[...] = xn


def _combine(o, p0, p1, route, x, mod, seq, final_w=None):
    n, d = x.shape
    tr = min(TR_COMBINE, seq)
    per_b = seq // tr
    in_specs = [pl.BlockSpec(memory_space=pl.ANY),
                pl.BlockSpec((tr, LANES), lambda i, p0, p1: (i, 0)),
                pl.BlockSpec((tr, d), lambda i, p0, p1: (i, 0)),
                pl.BlockSpec((None, 6, d), lambda i, p0, p1: (i // per_b, 0, 0))]
    args = [p0, p1, o, route, x, mod]
    if final_w is not None:
        in_specs.append(pl.BlockSpec((1, d), lambda i, p0, p1: (0, 0)))
        args.append(final_w)
    grid_spec = pltpu.PrefetchScalarGridSpec(
        num_scalar_prefetch=2,
        grid=(n // tr,),
        in_specs=in_specs,
        out_specs=pl.BlockSpec((tr, d), lambda i, p0, p1: (i, 0)),
        scratch_shapes=[pltpu.VMEM((2, tr, d), F32), pltpu.VMEM((2, tr, d), F32),
                        pltpu.SemaphoreType.DMA((2,))],
    )
    return pl.pallas_call(
        functools.partial(_combine_kernel, final_norm=final_w is not None),
        out_shape=jax.ShapeDtypeStruct((n, d), F32),
        grid_spec=grid_spec,
        compiler_params=_params(1),
        name="moe_combine",
    )(*args)


def _routing_tables(route, n_experts, tm):
    n = route.shape[0]
    i32 = jnp.int32
    ids = route[:, :TOP_K].astype(i32).reshape(-1)
    n_assign = n * TOP_K
    n_tiles = n_assign // tm + n_experts
    experts = jnp.arange(n_experts, dtype=i32)
    onehot = (ids[:, None] == experts[None, :]).astype(i32)
    ranks = jnp.cumsum(onehot, axis=0)
    counts = ranks[-1]
    offs = jnp.cumsum(counts) - counts
    tiles_per = (counts + tm - 1) // tm
    tile_end = jnp.cumsum(tiles_per)
    n_used = tile_end[-1]
    pad_offs = (tile_end - tiles_per) * tm
    pos = jnp.sum(onehot * (pad_offs[None, :] + ranks - 1), axis=1).reshape(n, TOP_K)
    tile_ids = jnp.arange(n_tiles, dtype=i32)
    tile_expert = jnp.sum((tile_ids[:, None] >= tile_end[None, :]).astype(i32), axis=1)
    last_expert = jnp.sum((n_used - 1 >= tile_end).astype(i32))
    tile_expert = jnp.where(tile_ids < n_used, tile_expert, last_expert).astype(i32)
    _, sorted_tok = lax.sort((ids, jnp.arange(n_assign, dtype=i32) // TOP_K), num_keys=1,
                             is_stable=True)
    row_expert = jnp.repeat(tile_expert, tm)
    rows = jnp.arange(n_tiles * tm, dtype=i32)
    shift = jnp.sum((row_expert[:, None] == experts[None, :]) * (offs - pad_offs)[None, :], axis=1)
    limit = jnp.sum((row_expert[:, None] == experts[None, :]) * (offs + counts)[None, :], axis=1)
    src = rows + shift
    valid = (src < limit) & (rows < n_used * tm)
    src_token = jnp.where(valid, sorted_tok[jnp.clip(src, 0, n_assign - 1)], 0).astype(i32)
    tile_rows = jnp.sum(valid.reshape(n_tiles, tm).astype(i32), axis=1)
    return (tile_expert, n_used.reshape(1).astype(i32), tile_rows, src_token,
            pos[:, 0].astype(i32), pos[:, 1].astype(i32))


def _final_norm_kernel(x_ref, w_ref, o_ref):
    x = x_ref[...]
    o_ref[...] = (x * lax.rsqrt(jnp.mean(x * x, axis=-1, keepdims=True) + EPS)) * w_ref[...]


def _final_norm(x, w):
    n, d = x.shape
    tm = min(TM_FFN, n)
    return pl.pallas_call(
        _final_norm_kernel,
        out_shape=jax.ShapeDtypeStruct((n, d), F32),
        grid=(n // tm,),
        in_specs=[pl.BlockSpec((tm, d), lambda i: (i, 0)), pl.BlockSpec((1, d), lambda i: (0, 0))],
        out_specs=pl.BlockSpec((tm, d), lambda i: (i, 0)),
        compiler_params=_params(1),
        name="final_norm",
    )(x, w)


def _retention_tables(ret_heads, seq):
    half = RET_HEAD_DIM // 2
    inv_freq = 1.0 / (ROPE_BASE ** (jnp.arange(half, dtype=F32) * 2.0 / RET_HEAD_DIM))
    ang = jnp.arange(seq, dtype=F32)[:, None] * inv_freq[None, :]
    log_gamma = jnp.log(1.0 - 2.0 ** (-5.0 - jnp.arange(ret_heads, dtype=F32)))
    j = jnp.arange(CHUNK, dtype=F32)
    diff = j[:, None] - j[None, :]
    dmask = jnp.where(diff >= 0, jnp.exp(log_gamma[:, None, None] * jnp.maximum(diff, 0.0)), 0.0)
    xi = jnp.exp(log_gamma[:, None] * (j[None, :] + 1.0))
    zeta = jnp.exp(log_gamma[:, None] * (CHUNK - 1.0 - j[None, :]))
    dec = jnp.exp(log_gamma * CHUNK)
    wide = (ret_heads, CHUNK, RET_HEAD_DIM)
    return (jnp.cos(ang), jnp.sin(ang), dmask,
            jnp.broadcast_to(xi[:, :, None], wide), jnp.broadcast_to(zeta[:, :, None], wide),
            jnp.broadcast_to(dec[:, None, None], (ret_heads, 1, RET_HEAD_DIM)))


def kernel(x, c, norm1_w, ada_w, ada_b, w_in, sg_w, sg_b, w_out, norm2_w, ffn_w_gate, ffn_w_up, ffn_w_down, router_w, router_b, moe_w_gate, moe_w_up, moe_w_down, final_norm_w):
    nb, seq, d = x.shape
    depth = w_in.shape[0]
    sg_heads = sg_w.shape[1]
    ret_heads = (w_in.shape[2] - 2 * sg_heads * SG_HEAD_DIM) // (4 * RET_HEAD_DIM)
    n = nb * seq
    assert seq % CHUNK == 0 and sg_w.shape[2] == CHUNK

    mod = _modulation(c, ada_w, ada_b).reshape(depth, nb, 6, d)
    cos, sin, dmask, xi, zeta, dec = _retention_tables(ret_heads, seq)
    xf = x.reshape(n, d)
    w_in_b, w_out_b = w_in.astype(BF16), w_out.astype(BF16)
    ffn_b = [w.astype(BF16) for w in (ffn_w_gate, ffn_w_up, ffn_w_down)]
    moe_wd_b = moe_w_down.astype(BF16)
    final_w = final_norm_w.reshape(1, d)
    for l in range(depth):
        z = _inproj(xf, norm1_w[l].reshape(1, d), mod[l], w_in_b, l, seq)
        sgb = jnp.broadcast_to(sg_b[l][:, :, None], (sg_heads, CHUNK, LANES))
        y = _mixer(z, cos, sin, sg_w[l], sgb, dmask, xi, zeta, dec, seq)
        i = l // 2
        nw2 = norm2_w[l].reshape(1, d)
        if l % 2 == 0:
            xf, h2 = _outproj(y, w_out_b, l, xf, mod[l], nw2, seq)
            xf = _ffn_dense(h2, *ffn_b, i, xf, mod[l], seq)
        else:
            n_experts = router_w.shape[2]
            tm = min(TM_FFN, n)
            xf, h2, route = _outproj(y, w_out_b, l, xf, mod[l], nw2, seq,
                                     router=(router_w[i], router_b[i]))
            tile_expert, n_used, tile_rows, src_token, p0, p1 = _routing_tables(route, n_experts, tm)
            o = _ffn_moe(tile_expert, n_used, tile_rows, src_token, h2, moe_w_gate, moe_w_up,
                         moe_wd_b, i, tm)
            xf = _combine(o, p0, p1, route, xf, mod[l], seq,
                          final_w=final_w if l == depth - 1 else None)
    if depth % 2 == 1:
        xf = _final_norm(xf, final_w)
    return xf.reshape(nb, seq, d)
```

```python
import functools

import jax
import jax.numpy as jnp
from jax import lax
from jax.experimental import pallas as pl
from jax.experimental.pallas import tpu as pltpu

F32 = jnp.float32
BF16 = jnp.bfloat16

EPS = 1e-6
ROPE_BASE = 10000.0
SG_HEAD_DIM = 128
RET_HEAD_DIM = 256
CHUNK = 128
TOP_K = 2
LANES = 128

TM_INPROJ = 512
TC_MIXER = 512
MIXER_CHUNK_UNROLL = 2
TM_OUTPROJ = 512
ROW_CHUNKS_OUTPROJ = 2
TM_FFN = 1024
TF_FFN = 512
TN_DOWN_DENSE = 512
TN_DOWN_MOE = 512
ROW_BLOCK_MOE = 256
GATHER_ROWS_PER_STEP = 128
TR_COMBINE = 512
TN_MOD = 512
DMA_ISSUE_UNROLL = 8
VMEM_LIMIT = 60 * 1024 * 1024


def _params(n_axes):
    return pltpu.CompilerParams(dimension_semantics=("arbitrary",) * n_axes,
                                vmem_limit_bytes=VMEM_LIMIT)


def _gelu_tanh(x):
    return 0.5 * x * (1.0 + jnp.tanh(0.7978845608028654 * (x + 0.044715 * (x * x * x))))


def _silu(x):
    return x * jax.nn.sigmoid(x)


def _pack_bf16_pairs(x):
    k = x.shape[1] // 2
    lo = lax.bitcast_convert_type(x[:, :k].astype(BF16).astype(F32), jnp.uint32)
    hi = lax.bitcast_convert_type(x[:, k:].astype(BF16).astype(F32), jnp.uint32)
    return (lo >> 16) | (hi & jnp.uint32(0xFFFF0000))


def _unpack_bf16_pairs(p):
    lo = lax.bitcast_convert_type(p << 16, F32)
    hi = lax.bitcast_convert_type(p & jnp.uint32(0xFFFF0000), F32)
    return jnp.concatenate([lo.astype(BF16), hi.astype(BF16)], axis=1)


def _rms_modulate(x, nw, shift, scale):
    y = x * lax.rsqrt(jnp.mean(x * x, axis=-1, keepdims=True) + EPS)
    return (y * nw) * (1.0 + scale) + shift


def _mod_kernel(cb_ref, w_ref, b_ref, o_ref):
    tn = w_ref.shape[1]
    for b in range(cb_ref.shape[0]):
        act = _silu(cb_ref[b])
        cols = [jnp.sum(w_ref[:, j * LANES:(j + 1) * LANES] * act, axis=0, keepdims=True)
                for j in range(tn // LANES)]
        o_ref[b:b + 1, :] = jnp.concatenate(cols, axis=1) + b_ref[...]


def _modulation(c, ada_w, ada_b):
    depth, d, n6 = ada_w.shape
    nb = c.shape[0]
    tn = min(TN_MOD, n6)
    cb = jnp.broadcast_to(c[:, :, None], (nb, d, LANES))
    return pl.pallas_call(
        _mod_kernel,
        out_shape=jax.ShapeDtypeStruct((depth, nb, n6), F32),
        grid=(depth, n6 // tn),
        in_specs=[pl.BlockSpec((nb, d, LANES), lambda l, j: (0, 0, 0)),
                  pl.BlockSpec((None, d, tn), lambda l, j: (l, 0, j)),
                  pl.BlockSpec((None, 1, tn), lambda l, j: (l, 0, j))],
        out_specs=pl.BlockSpec((None, nb, tn), lambda l, j: (l, 0, j)),
        compiler_params=_params(2),
        name="adaln_mod",
    )(cb, ada_w, ada_b.reshape(depth, 1, n6))


def _inproj_kernel(x_ref, nw_ref, mod_ref, w_ref, z_ref):
    h = _rms_modulate(x_ref[...], nw_ref[...], mod_ref[0:1, :], mod_ref[1:2, :])
    z_ref[...] = jnp.dot(h.astype(BF16), w_ref[...], preferred_element_type=F32).astype(z_ref.dtype)


def _inproj(x, nw, mod, w, layer, seq):
    n, d = x.shape
    nin = w.shape[2]
    tm = min(TM_INPROJ, seq)
    assert seq % tm == 0
    per_b = seq // tm
    return pl.pallas_call(
        _inproj_kernel,
        out_shape=jax.ShapeDtypeStruct((n, nin), BF16),
        grid=(n // tm,),
        in_specs=[pl.BlockSpec((tm, d), lambda i: (i, 0)),
                  pl.BlockSpec((1, d), lambda i: (0, 0)),
                  pl.BlockSpec((None, 6, d), lambda i: (i // per_b, 0, 0)),
                  pl.BlockSpec((None, d, nin), lambda i: (layer, 0, 0), pipeline_mode=pl.Buffered(1))],
        out_specs=pl.BlockSpec((tm, nin), lambda i: (i, 0)),
        compiler_params=_params(1),
        name="inproj",
    )(x, nw, mod, w)


def _mixer_kernel(z_ref, cos_ref, sin_ref, sgw_ref, sgb_ref, dmask_ref, xi_ref, zeta_ref,
                  dec_ref, y_ref, wm_ref, state_ref, *, sg_heads, ret_heads, steps_per_seq):
    sgw_w = sg_heads * SG_HEAD_DIM
    ret_w = ret_heads * RET_HEAD_DIM
    q_off, k_off = 2 * sgw_w, 2 * sgw_w + ret_w
    v_off, g_off = 2 * sgw_w + 2 * ret_w, 2 * sgw_w + 3 * ret_w
    half = RET_HEAD_DIM // 2
    step = pl.program_id(0)

    @pl.when(step == 0)
    def _():
        t = lax.broadcasted_iota(jnp.int32, (CHUNK, CHUNK), 0)
        s = lax.broadcasted_iota(jnp.int32, (CHUNK, CHUNK), 1)
        causal = (t >= s).astype(F32)
        for h in range(sg_heads):
            wm_ref[h] = (sgw_ref[h] * causal).astype(BF16)

    @pl.when(step % steps_per_seq == 0)
    def _():
        state_ref[...] = jnp.zeros_like(state_ref)

    def rotate(t, cos, sin):
        t1, t2 = t[:, :half], t[:, half:]
        return jnp.concatenate([t1 * cos - t2 * sin, t1 * sin + t2 * cos], axis=-1)

    def chunk(c, carry):
        rows = pl.ds(pl.multiple_of(c * CHUNK, CHUNK), CHUNK)
        cos = cos_ref[rows, :]
        sin = sin_ref[rows, :]
        for h in range(sg_heads):
            lo = h * SG_HEAD_DIM
            u = _gelu_tanh(z_ref[rows, lo:lo + SG_HEAD_DIM].astype(F32))
            v = _gelu_tanh(z_ref[rows, sgw_w + lo:sgw_w + lo + SG_HEAD_DIM].astype(F32))
            vc = v - jnp.mean(v, axis=-1, keepdims=True)
            vn = vc * lax.rsqrt(jnp.mean(vc * vc, axis=-1, keepdims=True) + EPS)
            sv = jnp.dot(wm_ref[h], vn.astype(BF16), preferred_element_type=F32) + sgb_ref[h]
            y_ref[rows, lo:lo + SG_HEAD_DIM] = (u * sv).astype(y_ref.dtype)
        for h in range(ret_heads):
            lo = h * RET_HEAD_DIM
            q = z_ref[rows, q_off + lo:q_off + lo + RET_HEAD_DIM].astype(F32)
            k = z_ref[rows, k_off + lo:k_off + lo + RET_HEAD_DIM].astype(F32)
            v = z_ref[rows, v_off + lo:v_off + lo + RET_HEAD_DIM]
            g = z_ref[rows, g_off + lo:g_off + lo + RET_HEAD_DIM].astype(F32)
            qr = rotate(q, cos, sin)
            kr = rotate(k, cos, sin) * (RET_HEAD_DIM ** -0.5)
            qb = qr.astype(BF16)
            scores = lax.dot_general(qb, kr.astype(BF16), (((1,), (1,)), ((), ())),
                                     preferred_element_type=F32) * dmask_ref[h]
            st = state_ref[h]
            yv = (jnp.dot(scores.astype(BF16), v, preferred_element_type=F32)
                  + jnp.dot(qb, st.astype(BF16), preferred_element_type=F32) * xi_ref[h])
            kz = (kr * zeta_ref[h]).astype(BF16)
            state_ref[h] = st * dec_ref[h] + lax.dot_general(
                kz, v, (((0,), (0,)), ((), ())), preferred_element_type=F32)
            yn = yv * lax.rsqrt(jnp.mean(yv * yv, axis=-1, keepdims=True) + EPS)
            y_ref[rows, sgw_w + lo:sgw_w + lo + RET_HEAD_DIM] = (_silu(g) * yn).astype(y_ref.dtype)
        return carry

    lax.fori_loop(0, z_ref.shape[0] // CHUNK, chunk, 0, unroll=MIXER_CHUNK_UNROLL)


def _mixer(z, cos, sin, sgw, sgb, dmask, xi, zeta, dec, seq):
    n, nin = z.shape
    sg_heads = sgw.shape[0]
    ret_heads = dmask.shape[0]
    mix = sg_heads * SG_HEAD_DIM + ret_heads * RET_HEAD_DIM
    tc = min(TC_MIXER, seq)
    per_seq = seq // tc
    const3 = lambda i: (0, 0, 0)
    kern = functools.partial(_mixer_kernel, sg_heads=sg_heads, ret_heads=ret_heads,
                             steps_per_seq=per_seq)
    return pl.pallas_call(
        kern,
        out_shape=jax.ShapeDtypeStruct((n, mix), BF16),
        grid=(n // tc,),
        in_specs=[pl.BlockSpec((tc, nin), lambda i: (i, 0)),
                  pl.BlockSpec((tc, LANES), lambda i: (i % per_seq, 0)),
                  pl.BlockSpec((tc, LANES), lambda i: (i % per_seq, 0)),
                  pl.BlockSpec(sgw.shape, const3),
                  pl.BlockSpec(sgb.shape, const3),
                  pl.BlockSpec(dmask.shape, const3),
                  pl.BlockSpec(xi.shape, const3),
                  pl.BlockSpec(zeta.shape, const3),
                  pl.BlockSpec(dec.shape, const3)],
        out_specs=pl.BlockSpec((tc, mix), lambda i: (i, 0)),
        scratch_shapes=[pltpu.VMEM((sg_heads, CHUNK, CHUNK), BF16),
                        pltpu.VMEM((ret_heads, RET_HEAD_DIM, RET_HEAD_DIM), F32)],
        compiler_params=_params(1),
        name="mixer",
    )(z, cos, sin, sgw, sgb, dmask, xi, zeta, dec)


def _route_top2(logits):
    lane = lax.broadcasted_iota(jnp.int32, logits.shape, 1)
    m1 = jnp.max(logits, axis=-1, keepdims=True)
    i1 = jnp.min(jnp.where(logits == m1, lane, LANES), axis=-1, keepdims=True)
    rest = jnp.where(lane == i1, -jnp.inf, logits)
    m2 = jnp.max(rest, axis=-1, keepdims=True)
    i2 = jnp.min(jnp.where(rest == m2, lane, LANES), axis=-1, keepdims=True)
    e = jnp.exp(m2 - m1)
    den = 1.0 + e
    w1 = 1.0 / den
    w2 = e / den
    return jnp.where(lane == 0, i1.astype(F32),
                     jnp.where(lane == 1, i2.astype(F32),
                               jnp.where(lane == 2, w1, jnp.where(lane == 3, w2, 0.0))))


def _router_logits(h2, rwt_ref, rb_ref, n_experts):
    lane = lax.broadcasted_iota(jnp.int32, (h2.shape[0], LANES), 1)
    logits = jnp.full((h2.shape[0], LANES), -jnp.inf, F32)
    for e in range(n_experts):
        s = jnp.sum(h2 * rwt_ref[e:e + 1, :], axis=-1, keepdims=True)
        logits = jnp.where(lane == e, s, logits)
    return logits + rb_ref[...]


def _outproj_kernel(*refs, n_experts, row_chunks):
    if n_experts:
        y_ref, w_ref, x_ref, mod_ref, nw_ref, rwt_ref, rb_ref, xo_ref, h2_ref, r_ref = refs
    else:
        y_ref, w_ref, x_ref, mod_ref, nw_ref, xo_ref, h2_ref = refs
    rc = y_ref.shape[0] // row_chunks
    for c in range(row_chunks):
        rows = slice(c * rc, (c + 1) * rc)
        yo = jnp.dot(y_ref[rows, :], w_ref[...], preferred_element_type=F32)
        xn = x_ref[rows, :] + mod_ref[2:3, :] * yo
        xo_ref[rows, :] = xn
        h2 = _rms_modulate(xn, nw_ref[...], mod_ref[3:4, :], mod_ref[4:5, :])
        if n_experts:
            h2_ref[rows] = _pack_bf16_pairs(h2).reshape(rc, -1, LANES)
            r_ref[rows, :] = _route_top2(_router_logits(h2, rwt_ref, rb_ref, n_experts))
        else:
            h2_ref[rows, :] = h2.astype(h2_ref.dtype)


def _outproj(y, w, layer, x, mod, nw, seq, router=None):
    n, d = x.shape
    mix = y.shape[1]
    tm = min(TM_OUTPROJ, seq)
    per_b = seq // tm
    row = lambda i: (i, 0)
    const = lambda i: (0, 0)
    in_specs = [pl.BlockSpec((tm, mix), row),
                pl.BlockSpec((None, mix, d), lambda i: (layer, 0, 0)),
                pl.BlockSpec((tm, d), row),
                pl.BlockSpec((None, 6, d), lambda i: (i // per_b, 0, 0)),
                pl.BlockSpec((1, d), const)]
    args = [y, w, x, mod, nw]
    out_shape = [jax.ShapeDtypeStruct((n, d), F32)]
    out_specs = [pl.BlockSpec((tm, d), row)]
    n_experts = 0
    if router is None:
        out_shape.append(jax.ShapeDtypeStruct((n, d), BF16))
        out_specs.append(pl.BlockSpec((tm, d), row))
    else:
        rw, rb = router
        n_experts = rw.shape[1]
        rb = jnp.pad(rb, (0, LANES - n_experts)).reshape(1, LANES)
        in_specs += [pl.BlockSpec((n_experts, d), const), pl.BlockSpec((1, LANES), const)]
        args += [rw.T, rb]
        slab = (d // 2 // LANES, LANES)
        out_shape += [jax.ShapeDtypeStruct((n, *slab), jnp.uint32),
                      jax.ShapeDtypeStruct((n, LANES), F32)]
        out_specs += [pl.BlockSpec((tm, *slab), lambda i: (i, 0, 0)),
                      pl.BlockSpec((tm, LANES), row)]
    return pl.pallas_call(
        functools.partial(_outproj_kernel, n_experts=n_experts, row_chunks=ROW_CHUNKS_OUTPROJ),
        out_shape=out_shape,
        grid=(n // tm,),
        in_specs=in_specs,
        out_specs=out_specs,
        compiler_params=_params(1),
        name="outproj_router" if n_experts else "outproj",
    )(*args)


def _ffn_up_step(h_ref, wg_ref, wu_ref, a_ref, j, tf, rows=None):
    rows = h_ref.shape[0] if rows is None else rows
    h = h_ref[:rows, :]
    g = jnp.dot(h, wg_ref[...].astype(BF16), preferred_element_type=F32)
    u = jnp.dot(h, wu_ref[...].astype(BF16), preferred_element_type=F32)
    a = (_silu(g) * u).astype(BF16)
    a_ref[:rows, pl.ds(pl.multiple_of(j * tf, tf), tf)] = a


def _ffn_down(a_ref, wd_ref, rows=None):
    rows = a_ref.shape[0] if rows is None else rows
    return jnp.dot(a_ref[:rows, :], wd_ref[...].astype(BF16), preferred_element_type=F32)


def _ffn_dense_kernel(h_ref, wg_ref, wu_ref, wd_ref, x_ref, mod_ref, o_ref, a_ref, *, nj, tf):
    j = pl.program_id(1)

    @pl.when(j < nj)
    def _():
        _ffn_up_step(h_ref, wg_ref, wu_ref, a_ref, j, tf)

    @pl.when(j >= nj)
    def _():
        o_ref[...] = x_ref[...] + mod_ref[5:6, :] * _ffn_down(a_ref, wd_ref)


def _ffn_dense(h2, wg, wu, wd, layer, x, mod, seq):
    n, d = x.shape
    f = wg.shape[2]
    tm = min(TM_FFN, seq)
    tf = min(TF_FFN, f)
    tn = min(TN_DOWN_DENSE, d)
    nj, nd = f // tf, d // tn
    per_b = seq // tm
    feat = lambda j: jnp.minimum(j, nj - 1)
    col = lambda j: jnp.maximum(j - nj, 0)
    return pl.pallas_call(
        functools.partial(_ffn_dense_kernel, nj=nj, tf=tf),
        out_shape=jax.ShapeDtypeStruct((n, d), F32),
        grid=(n // tm, nj + nd),
        in_specs=[pl.BlockSpec((tm, d), lambda i, j: (i, 0)),
                  pl.BlockSpec((None, d, tf), lambda i, j: (layer, 0, feat(j))),
                  pl.BlockSpec((None, d, tf), lambda i, j: (layer, 0, feat(j))),
                  pl.BlockSpec((None, f, tn), lambda i, j: (layer, 0, col(j))),
                  pl.BlockSpec((tm, tn), lambda i, j: (i, col(j))),
                  pl.BlockSpec((None, 6, tn), lambda i, j: (i // per_b, 0, col(j)))],
        out_specs=pl.BlockSpec((tm, tn), lambda i, j: (i, col(j))),
        scratch_shapes=[pltpu.VMEM((tm, f), BF16)],
        compiler_params=_params(2),
        name="ffn_dense",
    )(h2, wg, wu, wd, x, mod)


def _row_copy(src_hbm, dst_ref, src_row, dst_row, sem):
    return pltpu.make_async_copy(src_hbm.at[pl.ds(src_row, 1), :],
                                 dst_ref.at[pl.ds(dst_row, 1), :], sem)


def _ffn_moe_kernel(te_ref, nu_ref, nv_ref, src_ref, hp_hbm, wg_ref, wu_ref, wd_ref, o_ref,
                    a_ref, h_ref, stage_ref, sem, *, nj, tf):
    i = pl.program_id(0)
    j = pl.program_id(1)
    tm = h_ref.shape[0]
    n_used = nu_ref[0]
    active = i < n_used
    blocks = (nv_ref[i] + ROW_BLOCK_MOE - 1) // ROW_BLOCK_MOE

    def start_rows(tile, first, count):
        def body(k, carry):
            r = first + k
            pltpu.make_async_copy(hp_hbm.at[src_ref[tile * tm + r]], stage_ref.at[r], sem).start()
            return carry
        lax.fori_loop(0, count, body, 0, unroll=DMA_ISSUE_UNROLL)

    @pl.when(jnp.logical_and(i == 0, j == 0))
    def _():
        start_rows(0, 0, tm)

    @pl.when(jnp.logical_and(active, j == 0))
    def _():
        pltpu.make_async_copy(hp_hbm.at[pl.ds(0, tm)], stage_ref, sem).wait()
        h_ref[...] = _unpack_bf16_pairs(stage_ref[...].reshape(tm, -1))

    prefetching = jnp.logical_and(j >= 1, j <= tm // GATHER_ROWS_PER_STEP)

    @pl.when(jnp.logical_and(i + 1 < n_used, prefetching))
    def _():
        start_rows(i + 1, (j - 1) * GATHER_ROWS_PER_STEP, GATHER_ROWS_PER_STEP)

    for q in range(1, tm // ROW_BLOCK_MOE + 1):
        rows = q * ROW_BLOCK_MOE
        this = jnp.logical_and(active, blocks == q)

        @pl.when(jnp.logical_and(this, j < nj))
        def _():
            _ffn_up_step(h_ref, wg_ref, wu_ref, a_ref, j, tf, rows=rows)

        @pl.when(jnp.logical_and(this, j >= nj))
        def _():
            o_ref[:rows, :] = _ffn_down(a_ref, wd_ref, rows=rows)
            if rows < tm:
                o_ref[rows:, :] = jnp.zeros((tm - rows, o_ref.shape[1]), o_ref.dtype)

    @pl.when(jnp.logical_and(jnp.logical_not(active), j >= nj))
    def _():
        o_ref[...] = jnp.zeros_like(o_ref)


def _ffn_moe(tile_expert, n_used, tile_rows, src_token, hp, wg, wu, wd, layer, tm):
    r = src_token.shape[0]
    d, f = wg.shape[2:]
    tf = min(TF_FFN, f)
    tn = min(TN_DOWN_MOE, d)
    nj, nd = f // tf, d // tn
    assert tm % ROW_BLOCK_MOE == 0 and tm % GATHER_ROWS_PER_STEP == 0
    assert tm // GATHER_ROWS_PER_STEP <= nj + nd - 1

    def feat(i, j, nu):
        return jnp.where(i < nu[0], jnp.minimum(j, nj - 1), nj - 1)

    def col_in(i, j, nu):
        return jnp.where(i < nu[0], jnp.maximum(j - nj, 0), nd - 1)

    up_spec = pl.BlockSpec((None, None, d, tf),
                           lambda i, j, te, nu, nv, src: (layer, te[i], 0, feat(i, j, nu)))
    grid_spec = pltpu.PrefetchScalarGridSpec(
        num_scalar_prefetch=4,
        grid=(r // tm, nj + nd),
        in_specs=[pl.BlockSpec(memory_space=pl.ANY),
                  up_spec, up_spec,
                  pl.BlockSpec((None, None, f, tn),
                               lambda i, j, te, nu, nv, src: (layer, te[i], 0, col_in(i, j, nu)))],
        out_specs=pl.BlockSpec((tm, tn), lambda i, j, te, nu, nv, src: (i, jnp.maximum(j - nj, 0))),
        scratch_shapes=[pltpu.VMEM((tm, f), BF16), pltpu.VMEM((tm, d), BF16),
                        pltpu.VMEM((tm, *hp.shape[1:]), jnp.uint32), pltpu.SemaphoreType.DMA],
    )
    return pl.pallas_call(
        functools.partial(_ffn_moe_kernel, nj=nj, tf=tf),
        out_shape=jax.ShapeDtypeStruct((r, d), F32),
        grid_spec=grid_spec,
        compiler_params=_params(2),
        name="ffn_moe",
    )(tile_expert, n_used, tile_rows, src_token, hp, wg, wu, wd)


def _combine_kernel(*refs, final_norm):
    if final_norm:
        p0_ref, p1_ref, o_hbm, route_ref, x_ref, mod_ref, fw_ref, out_ref, buf0_ref, buf1_ref, sems = refs
    else:
        p0_ref, p1_ref, o_hbm, route_ref, x_ref, mod_ref, out_ref, buf0_ref, buf1_ref, sems = refs
    rows = buf0_ref.shape[1]
    step = pl.program_id(0)
    slot = step % 2

    def start_rows(of_step, into):
        def issue(k, carry):
            t = of_step * rows + k
            _row_copy(o_hbm, buf0_ref.at[into], p0_ref[t], k, sems.at[into]).start()
            _row_copy(o_hbm, buf1_ref.at[into], p1_ref[t], k, sems.at[into]).start(priority=1)
            return carry
        lax.fori_loop(0, rows, issue, 0, unroll=DMA_ISSUE_UNROLL)

    @pl.when(step == 0)
    def _():
        start_rows(0, 0)

    @pl.when(step + 1 < pl.num_programs(0))
    def _():
        start_rows(step + 1, 1 - slot)

    pltpu.make_async_copy(o_hbm.at[pl.ds(0, rows), :], buf0_ref.at[slot], sems.at[slot]).wait()
    pltpu.make_async_copy(o_hbm.at[pl.ds(0, rows), :], buf1_ref.at[slot], sems.at[slot]).wait()
    w0 = route_ref[:, TOP_K:TOP_K + 1]
    w1 = route_ref[:, TOP_K + 1:TOP_K + 2]
    xn = x_ref[...] + mod_ref[5:6, :] * (w0 * buf0_ref[slot] + w1 * buf1_ref[slot])
    if final_norm:
        xn = (xn * lax.rsqrt(jnp.mean(xn * xn, axis=-1, keepdims=True) + EPS)) * fw_ref[...]
    out_ref[...] = xn


def _combine(o, p0, p1, route, x, mod, seq, final_w=None):
    n, d = x.shape
    tr = min(TR_COMBINE, seq)
    per_b = seq // tr
    in_specs = [pl.BlockSpec(memory_space=pl.ANY),
                pl.BlockSpec((tr, LANES), lambda i, p0, p1: (i, 0)),
                pl.BlockSpec((tr, d), lambda i, p0, p1: (i, 0)),
                pl.BlockSpec((None, 6, d), lambda i, p0, p1: (i // per_b, 0, 0))]
    args = [p0, p1, o, route, x, mod]
    if final_w is not None:
        in_specs.append(pl.BlockSpec((1, d), lambda i, p0, p1: (0, 0)))
        args.append(final_w)
    grid_spec = pltpu.PrefetchScalarGridSpec(
        num_scalar_prefetch=2,
        grid=(n // tr,),
        in_specs=in_specs,
        out_specs=pl.BlockSpec((tr, d), lambda i, p0, p1: (i, 0)),
        scratch_shapes=[pltpu.VMEM((2, tr, d), F32), pltpu.VMEM((2, tr, d), F32),
                        pltpu.SemaphoreType.DMA((2,))],
    )
    return pl.pallas_call(
        functools.partial(_combine_kernel, final_norm=final_w is not None),
        out_shape=jax.ShapeDtypeStruct((n, d), F32),
        grid_spec=grid_spec,
        compiler_params=_params(1),
        name="moe_combine",
    )(*args)


def _routing_tables(route, n_experts, tm):
    n = route.shape[0]
    i32 = jnp.int32
    ids = route[:, :TOP_K].astype(i32).reshape(-1)
    n_assign = n * TOP_K
    n_tiles = n_assign // tm + n_experts
    experts = jnp.arange(n_experts, dtype=i32)
    onehot = (ids[:, None] == experts[None, :]).astype(i32)
    ranks = jnp.cumsum(onehot, axis=0)
    counts = ranks[-1]
    offs = jnp.cumsum(counts) - counts
    tiles_per = (counts + tm - 1) // tm
    tile_end = jnp.cumsum(tiles_per)
    n_used = tile_end[-1]
    pad_offs = (tile_end - tiles_per) * tm
    pos = jnp.sum(onehot * (pad_offs[None, :] + ranks - 1), axis=1).reshape(n, TOP_K)
    tile_ids = jnp.arange(n_tiles, dtype=i32)
    tile_expert = jnp.sum((tile_ids[:, None] >= tile_end[None, :]).astype(i32), axis=1)
    last_expert = jnp.sum((n_used - 1 >= tile_end).astype(i32))
    tile_expert = jnp.where(tile_ids < n_used, tile_expert, last_expert).astype(i32)
    _, sorted_tok = lax.sort((ids, jnp.arange(n_assign, dtype=i32) // TOP_K), num_keys=1,
                             is_stable=True)
    row_expert = jnp.repeat(tile_expert, tm)
    rows = jnp.arange(n_tiles * tm, dtype=i32)
    shift = jnp.sum((row_expert[:, None] == experts[None, :]) * (offs - pad_offs)[None, :], axis=1)
    limit = jnp.sum((row_expert[:, None] == experts[None, :]) * (offs + counts)[None, :], axis=1)
    src = rows + shift
    valid = (src < limit) & (rows < n_used * tm)
    src_token = jnp.where(valid, sorted_tok[jnp.clip(src, 0, n_assign - 1)], 0).astype(i32)
    tile_rows = jnp.sum(valid.reshape(n_tiles, tm).astype(i32), axis=1)
    return (tile_expert, n_used.reshape(1).astype(i32), tile_rows, src_token,
            pos[:, 0].astype(i32), pos[:, 1].astype(i32))


def _final_norm_kernel(x_ref, w_ref, o_ref):
    x = x_ref[...]
    o_ref[...] = (x * lax.rsqrt(jnp.mean(x * x, axis=-1, keepdims=True) + EPS)) * w_ref[...]


def _final_norm(x, w):
    n, d = x.shape
    tm = min(TM_FFN, n)
    return pl.pallas_call(
        _final_norm_kernel,
        out_shape=jax.ShapeDtypeStruct((n, d), F32),
        grid=(n // tm,),
        in_specs=[pl.BlockSpec((tm, d), lambda i: (i, 0)), pl.BlockSpec((1, d), lambda i: (0, 0))],
        out_specs=pl.BlockSpec((tm, d), lambda i: (i, 0)),
        compiler_params=_params(1),
        name="final_norm",
    )(x, w)


def _retention_tables(ret_heads, seq):
    half = RET_HEAD_DIM // 2
    inv_freq = 1.0 / (ROPE_BASE ** (jnp.arange(half, dtype=F32) * 2.0 / RET_HEAD_DIM))
    ang = jnp.arange(seq, dtype=F32)[:, None] * inv_freq[None, :]
    log_gamma = jnp.log(1.0 - 2.0 ** (-5.0 - jnp.arange(ret_heads, dtype=F32)))
    j = jnp.arange(CHUNK, dtype=F32)
    diff = j[:, None] - j[None, :]
    dmask = jnp.where(diff >= 0, jnp.exp(log_gamma[:, None, None] * jnp.maximum(diff, 0.0)), 0.0)
    xi = jnp.exp(log_gamma[:, None] * (j[None, :] + 1.0))
    zeta = jnp.exp(log_gamma[:, None] * (CHUNK - 1.0 - j[None, :]))
    dec = jnp.exp(log_gamma * CHUNK)
    wide = (ret_heads, CHUNK, RET_HEAD_DIM)
    return (jnp.cos(ang), jnp.sin(ang), dmask,
            jnp.broadcast_to(xi[:, :, None], wide), jnp.broadcast_to(zeta[:, :, None], wide),
            jnp.broadcast_to(dec[:, None, None], (ret_heads, 1, RET_HEAD_DIM)))


def kernel(x, c, norm1_w, ada_w, ada_b, w_in, sg_w, sg_b, w_out, norm2_w, ffn_w_gate, ffn_w_up, ffn_w_down, router_w, router_b, moe_w_gate, moe_w_up, moe_w_down, final_norm_w):
    nb, seq, d = x.shape
    depth = w_in.shape[0]
    sg_heads = sg_w.shape[1]
    ret_heads = (w_in.shape[2] - 2 * sg_heads * SG_HEAD_DIM) // (4 * RET_HEAD_DIM)
    n = nb * seq
    assert seq % CHUNK == 0 and sg_w.shape[2] == CHUNK

    mod = _modulation(c, ada_w, ada_b).reshape(depth, nb, 6, d)
    cos, sin, dmask, xi, zeta, dec = _retention_tables(ret_heads, seq)
    xf = x.reshape(n, d)
    w_in_b, w_out_b = w_in.astype(BF16), w_out.astype(BF16)
    ffn_b = [w.astype(BF16) for w in (ffn_w_gate, ffn_w_up, ffn_w_down)]
    moe_wd_b = moe_w_down.astype(BF16)
    final_w = final_norm_w.reshape(1, d)
    for l in range(depth):
        z = _inproj(xf, norm1_w[l].reshape(1, d), mod[l], w_in_b, l, seq)
        sgb = jnp.broadcast_to(sg_b[l][:, :, None], (sg_heads, CHUNK, LANES))
        y = _mixer(z, cos, sin, sg_w[l], sgb, dmask, xi, zeta, dec, seq)
        i = l // 2
        nw2 = norm2_w[l].reshape(1, d)
        if l % 2 == 0:
            xf, h2 = _outproj(y, w_out_b, l, xf, mod[l], nw2, seq)
            xf = _ffn_dense(h2, *ffn_b, i, xf, mod[l], seq)
        else:
            n_experts = router_w.shape[2]
            tm = min(TM_FFN, n)
            xf, h2, route = _outproj(y, w_out_b, l, xf, mod[l], nw2, seq,
                                     router=(router_w[i], router_b[i]))
            tile_expert, n_used, tile_rows, src_token, p0, p1 = _routing_tables(route, n_experts, tm)
            o = _ffn_moe(tile_expert, n_used, tile_rows, src_token, h2, moe_w_gate, moe_w_up,
                         moe_wd_b, i, tm)
            xf = _combine(o, p0, p1, route, xf, mod[l], seq,
                          final_w=final_w if l == depth - 1 else None)
    if depth % 2 == 1:
        xf = _final_norm(xf, final_w)
    return xf.reshape(nb, seq, d)
```
